```python
import jax, jax.numpy as jnp
from jax import lax
import numpy as np

D_MODEL = 2048
BATCH = 4
SEQ = 8192
DEPTH = 1
DEC_BATCH = 8
DEC_SEQ = 64
PAST_LEN = 2048

CHUNK = 64
EPS = 1e-6
M_EXPAND = 2
M_D_INNER = M_EXPAND * D_MODEL
M_HEAD_DIM = 64
M_N_HEADS = M_D_INNER // M_HEAD_DIM
M_N_GROUPS = 8
M_D_STATE = 128
M_D_CONV = 4
M_CONV_DIM = M_D_INNER + 2 * M_N_GROUPS * M_D_STATE
F_N_HEADS = 16
F_HEAD_DIM = 128
F_WIDTH = F_N_HEADS * F_HEAD_DIM
Q_BLOCK = 128
D_FF = 5504
FFN_CONV = 3
N_MOD = 6
IN_SIZES = (M_D_INNER, M_CONV_DIM, M_N_HEADS, F_WIDTH, F_WIDTH, F_WIDTH, F_N_HEADS, D_MODEL, D_MODEL)
IN_COLS = M_D_INNER + M_CONV_DIM + M_N_HEADS + 3 * F_WIDTH + F_N_HEADS + 2 * D_MODEL

kernel_name = "hybrid_ssd_fox_convffn_stream_step"


def _rms(x):
    xf = x.astype(jnp.float32)
    return (xf * lax.rsqrt(jnp.mean(xf * xf, axis=-1, keepdims=True) + EPS)).astype(x.dtype)


def _split_cols(t, sizes):
    outs, off = [], 0
    for s in sizes:
        outs.append(t[..., off:off + s])
        off += s
    return outs


def _causal_dwconv(u, buf, w, b):
    width = w.shape[0]
    L = u.shape[1]
    up = jnp.concatenate([buf.astype(u.dtype), u], axis=1)
    y = b
    for j in range(width):
        y = y + up[:, j:j + L] * w[j]
    return y, up[:, L:]


def _ssd(x, dt, a, bm, cm, h0):
    bsz, L, H, P = x.shape
    G, N = bm.shape[2], bm.shape[3]
    Hg = H // G
    Q = min(CHUNK, L)
    nc = L // Q

    def to_chunks(t):
        return jnp.moveaxis(t.reshape((bsz, nc, Q) + t.shape[2:]), 1, 0)

    xs = (to_chunks(x.reshape(bsz, L, G, Hg, P)), to_chunks(dt.reshape(bsz, L, G, Hg)),
          to_chunks(bm), to_chunks(cm))
    causal = jnp.tril(jnp.ones((Q, Q), dtype=bool))
    a_g = a.astype(jnp.float32).reshape(G, Hg)

    def step(h, inp):
        xc, dtc, bc, cc = inp
        cum = jnp.cumsum(dtc.astype(jnp.float32) * a_g, axis=1)
        seg = cum[:, :, None] - cum[:, None, :]
        lmat = jnp.exp(jnp.where(causal[None, :, :, None, None], seg, -jnp.inf))
        xdt = xc * dtc[..., None]
        cb = jnp.einsum('bqgn,bsgn->bqsg', cc, bc)
        y = jnp.einsum('bqsg,bqsgh,bsghp->bqghp', cb, lmat, xdt)
        y = y + jnp.einsum('bqgn,bghpn->bqghp', cc, h) * jnp.exp(cum)[..., None]
        decay_end = jnp.exp(cum[:, -1:] - cum)
        h = h * jnp.exp(cum[:, -1])[..., None, None] + jnp.einsum('bsgn,bsgh,bsghp->bghpn', bc, decay_end, xdt)
        return h, y.astype(jnp.float32)

    h, ys = lax.scan(step, h0.astype(jnp.float32).reshape(bsz, G, Hg, P, N), xs)
    y = jnp.moveaxis(ys, 0, 1).reshape(bsz, L, H, P).astype(x.dtype)
    return y, h.reshape(bsz, H, P, N).astype(h0.dtype)


def _fox_attention(q, k, v, cum_q, cum_k, pos_q, pos_k):
    bsz, T, H, Dh = q.shape
    QB = min(Q_BLOCK, T)
    nb = T // QB
    scale = Dh ** -0.5
    ck = jnp.swapaxes(cum_k, 1, 2)

    def blocks(t):
        return jnp.moveaxis(t.reshape((bsz, nb, QB) + t.shape[2:]), 1, 0)

    def one(args):
        qi, cqi, pi = args
        s = jnp.einsum('bqhd,bkhd->bhqk', qi, k).astype(jnp.float32) * scale
        s = s + jnp.swapaxes(cqi, 1, 2)[..., None] - ck[:, :, None, :]
        mask = pos_k[None, :] <= pi[:, None]
        p = jax.nn.softmax(jnp.where(mask, s, -jnp.inf), axis=-1)
        return jnp.einsum('bhqk,bkhd->bqhd', p.astype(v.dtype), v)

    o = lax.map(one, (blocks(q), blocks(cum_q), pos_q.reshape(nb, QB)))
    return jnp.moveaxis(o, 0, 1).reshape(bsz, T, H, Dh)


def _layer(x, c, k_past, v_past, logf_past, ssm_h0, mconv_buf, ffn_buf, p):
    bsz, T, _ = x.shape
    p0 = k_past.shape[1]
    mod = jnp.einsum('bd,de->be', jax.nn.silu(c), p['w_ada']) + p['b_ada']
    sh1, sc1, g1, sh2, sc2, g2 = jnp.split(mod, N_MOD, axis=-1)
    h = _rms(x) * p['norm1_w'] * (1 + sc1[:, None]) + sh1[:, None]
    proj = jnp.einsum('btd,de->bte', h, p['w_in'])
    z, xbc, dt_raw, q, k, v, f_raw, gm_raw, gf_raw = _split_cols(proj, IN_SIZES)

    xbc, new_mconv = _causal_dwconv(xbc, mconv_buf, p['m_conv_w'], p['m_conv_b'])
    xbc = jax.nn.silu(xbc)
    xm, bm, cm = _split_cols(xbc, (M_D_INNER, M_N_GROUPS * M_D_STATE, M_N_GROUPS * M_D_STATE))
    xm = xm.reshape(bsz, T, M_N_HEADS, M_HEAD_DIM)
    dt = jax.nn.softplus(dt_raw + p['m_dt_bias'])
    a = -jnp.exp(p['m_a_log'])
    ym, new_h = _ssd(xm, dt, a, bm.reshape(bsz, T, M_N_GROUPS, M_D_STATE),
                     cm.reshape(bsz, T, M_N_GROUPS, M_D_STATE), ssm_h0)
    ym = (ym + xm * p['m_d'][:, None]).reshape(bsz, T, M_D_INNER) * jax.nn.silu(z)
    ym = _rms(ym.reshape(bsz, T, M_N_GROUPS, M_D_INNER // M_N_GROUPS)).reshape(bsz, T, M_D_INNER) * p['m_norm_w']
    ym = jnp.einsum('bte,ed->btd', ym, p['w_proj_m'])

    q = _rms(q.reshape(bsz, T, F_N_HEADS, F_HEAD_DIM)) * p['q_norm_w']
    k = _rms(k.reshape(bsz, T, F_N_HEADS, F_HEAD_DIM)) * p['k_norm_w']
    v = v.reshape(bsz, T, F_N_HEADS, F_HEAD_DIM)
    logf = jax.nn.log_sigmoid((f_raw + p['f_bias']).astype(jnp.float32))
    k_all = jnp.concatenate([k_past.astype(k.dtype), k], axis=1)
    v_all = jnp.concatenate([v_past.astype(v.dtype), v], axis=1)
    cum_all = jnp.cumsum(jnp.concatenate([logf_past.astype(jnp.float32), logf], axis=1), axis=1)
    pos_k = jnp.arange(p0 + T)
    yf = _fox_attention(q, k_all, v_all, cum_all[:, p0:], cum_all, p0 + jnp.arange(T), pos_k)
    yf = jnp.einsum('bte,ed->btd', yf.reshape(bsz, T, F_WIDTH), p['w_proj_f'])

    mix = jax.nn.sigmoid(gm_raw) * ym + jax.nn.sigmoid(gf_raw) * yf
    x = x + g1[:, None] * jnp.einsum('btd,de->bte', mix, p['w_out'])

    h2 = _rms(x) * p['norm2_w'] * (1 + sc2[:, None]) + sh2[:, None]
    u = jnp.einsum('btd,de->bte', h2, p['w_up'])
    u, new_ffn = _causal_dwconv(u, ffn_buf, p['ffn_conv_w'], p['ffn_conv_b'])
    ua, ub = _split_cols(u, (D_FF, D_FF))
    x = x + g2[:, None] * jnp.einsum('btf,fd->btd', jax.nn.silu(ua) * ub, p['w_down'])
    return x, k, v, logf.astype(x.dtype), new_h, new_mconv, new_ffn


def setup_inputs(seed: int = 0) -> dict:
    key = jax.random.key(seed)
    ks = iter(jax.random.split(key, 40))

    def nrm(shape, s=1.0):
        return s * jax.random.normal(next(ks), shape, jnp.float32)

    def uni(shape, lo, hi):
        return jax.random.uniform(next(ks), shape, jnp.float32, lo, hi)

    dt0 = jnp.exp(uni((DEPTH, M_N_HEADS), np.log(1e-3), np.log(1e-1)))
    return {
        "x_prompt": nrm((BATCH, SEQ, D_MODEL)),
        "x_sample": nrm((DEC_BATCH, DEC_SEQ, D_MODEL)),
        "c_prompt": nrm((BATCH, D_MODEL)),
        "c_sample": nrm((DEC_BATCH, D_MODEL)),
        "cache_fox_k": nrm((DEPTH, DEC_BATCH, PAST_LEN, F_N_HEADS, F_HEAD_DIM)),
        "cache_fox_v": nrm((DEPTH, DEC_BATCH, PAST_LEN, F_N_HEADS, F_HEAD_DIM)),
        "cache_fox_logf": jax.nn.log_sigmoid(2.0 + nrm((DEPTH, DEC_BATCH, PAST_LEN, F_N_HEADS))),
        "state_ssm": nrm((DEPTH, DEC_BATCH, M_N_HEADS, M_HEAD_DIM, M_D_STATE), 0.1),
        "state_mamba_conv": nrm((DEPTH, DEC_BATCH, M_D_CONV - 1, M_CONV_DIM)),
        "state_ffn_conv": nrm((DEPTH, DEC_BATCH, FFN_CONV - 1, 2 * D_FF)),
        "norm1_w": 1.0 + nrm((DEPTH, D_MODEL), 0.02),
        "norm2_w": 1.0 + nrm((DEPTH, D_MODEL), 0.02),
        "w_ada": nrm((DEPTH, D_MODEL, N_MOD * D_MODEL), 0.5 * D_MODEL ** -0.5),
        "b_ada": nrm((DEPTH, N_MOD * D_MODEL), 0.02),
        "w_in": nrm((DEPTH, D_MODEL, IN_COLS), D_MODEL ** -0.5),
        "m_conv_w": nrm((DEPTH, M_D_CONV, M_CONV_DIM), M_D_CONV ** -0.5),
        "m_conv_b": nrm((DEPTH, M_CONV_DIM), 0.02),
        "m_dt_bias": dt0 + jnp.log(-jnp.expm1(-dt0)),
        "m_a_log": jnp.log(uni((DEPTH, M_N_HEADS), 1.0, 16.0)),
        "m_d": 1.0 + nrm((DEPTH, M_N_HEADS), 0.02),
        "m_norm_w": 1.0 + nrm((DEPTH, M_D_INNER), 0.02),
        "f_bias": 2.0 + nrm((DEPTH, F_N_HEADS), 0.1),
        "q_norm_w": 1.0 + nrm((DEPTH, F_HEAD_DIM), 0.02),
        "k_norm_w": 1.0 + nrm((DEPTH, F_HEAD_DIM), 0.02),
        "w_proj_m": nrm((DEPTH, M_D_INNER, D_MODEL), M_D_INNER ** -0.5),
        "w_proj_f": nrm((DEPTH, F_WIDTH, D_MODEL), F_WIDTH ** -0.5),
        "w_out": nrm((DEPTH, D_MODEL, D_MODEL), D_MODEL ** -0.5),
        "w_up": nrm((DEPTH, D_MODEL, 2 * D_FF), D_MODEL ** -0.5),
        "ffn_conv_w": nrm((DEPTH, FFN_CONV, 2 * D_FF), FFN_CONV ** -0.5),
        "ffn_conv_b": nrm((DEPTH, 2 * D_FF), 0.02),
        "w_down": nrm((DEPTH, D_FF, D_MODEL), D_FF ** -0.5),
    }


def reference(x_prompt, x_sample, c_prompt, c_sample, cache_fox_k, cache_fox_v, cache_fox_logf,
              state_ssm, state_mamba_conv, state_ffn_conv, norm1_w, norm2_w, w_ada, b_ada, w_in,
              m_conv_w, m_conv_b, m_dt_bias, m_a_log, m_d, m_norm_w, f_bias, q_norm_w, k_norm_w,
              w_proj_m, w_proj_f, w_out, w_up, ffn_conv_w, ffn_conv_b, w_down):
    bp, dtp = x_prompt.shape[0], x_prompt.dtype
    xp, xs = x_prompt, x_sample
    outs_p = [[] for _ in range(6)]
    outs_s = [[] for _ in range(6)]
    for l in range(DEPTH):
        p = {
            'norm1_w': norm1_w[l], 'norm2_w': norm2_w[l], 'w_ada': w_ada[l], 'b_ada': b_ada[l],
            'w_in': w_in[l], 'm_conv_w': m_conv_w[l], 'm_conv_b': m_conv_b[l], 'm_dt_bias': m_dt_bias[l],
            'm_a_log': m_a_log[l], 'm_d': m_d[l], 'm_norm_w': m_norm_w[l], 'f_bias': f_bias[l],
            'q_norm_w': q_norm_w[l], 'k_norm_w': k_norm_w[l], 'w_proj_m': w_proj_m[l], 'w_proj_f': w_proj_f[l],
            'w_out': w_out[l], 'w_up': w_up[l], 'ffn_conv_w': ffn_conv_w[l], 'ffn_conv_b': ffn_conv_b[l],
            'w_down': w_down[l],
        }
        res_p = _layer(xp, c_prompt,
                       jnp.zeros((bp, 0, F_N_HEADS, F_HEAD_DIM), dtp), jnp.zeros((bp, 0, F_N_HEADS, F_HEAD_DIM), dtp),
                       jnp.zeros((bp, 0, F_N_HEADS), dtp), jnp.zeros((bp, M_N_HEADS, M_HEAD_DIM, M_D_STATE), dtp),
                       jnp.zeros((bp, M_D_CONV - 1, M_CONV_DIM), dtp), jnp.zeros((bp, FFN_CONV - 1, 2 * D_FF), dtp), p)
        res_s = _layer(xs, c_sample, cache_fox_k[l], cache_fox_v[l], cache_fox_logf[l], state_ssm[l],
                       state_mamba_conv[l], state_ffn_conv[l], p)
        xp, xs = res_p[0], res_s[0]
        for i in range(6):
            outs_p[i].append(res_p[i + 1])
            outs_s[i].append(res_s[i + 1])
    k_p, v_p, lf_p, ssm_p, mc_p, fc_p = [jnp.stack(o, axis=0) for o in outs_p]
    k_s, v_s, lf_s, ssm_s, mc_s, fc_s = [jnp.stack(o, axis=0) for o in outs_s]
    return (xp, xs, k_p, v_p, lf_p, ssm_p, mc_p, fc_p, k_s, v_s, lf_s, ssm_s, mc_s, fc_s)
```

```python
import functools

import jax
import jax.numpy as jnp
from jax import lax
from jax.experimental import pallas as pl
from jax.experimental.pallas import tpu as pltpu

F32, BF16 = jnp.float32, jnp.bfloat16

D_MODEL = 2048
EPS = 1e-6
N_MOD = 6
SSD_INNER = 4096
SSD_HEADS = 64
SSD_HEAD_DIM = 64
SSD_GROUPS = 8
SSD_STATE = 128
SSD_CONV_W = 4
SSD_CONV_DIM = SSD_INNER + 2 * SSD_GROUPS * SSD_STATE
SSD_GROUP_W = SSD_INNER // SSD_GROUPS
SSD_HEADS_PER_GROUP = SSD_HEADS // SSD_GROUPS
ATT_HEADS = 16
ATT_HEAD_DIM = 128
ATT_WIDTH = ATT_HEADS * ATT_HEAD_DIM
ATT_SCALE = ATT_HEAD_DIM ** -0.5
D_FF = 5504
D_FF_PAD = 5632
FFN_CONV_W = 3
LANES = 128
SUBLANES_F32 = 8
SUBLANES_BF16 = 16
VMEM_LIMIT_BYTES = 56 * 1024 * 1024

DT_LANES = SSD_HEADS
LOGF_LANE0 = SSD_HEADS
NEG_BIG = -1e30


def _cparams(semantics):
    return pltpu.CompilerParams(dimension_semantics=semantics, vmem_limit_bytes=VMEM_LIMIT_BYTES)


def _silu(v):
    return v * jax.nn.sigmoid(v)


def _split3(v):
    hi = v.astype(BF16)
    r1 = v - hi.astype(F32)
    mid = r1.astype(BF16)
    lo = (r1 - mid.astype(F32)).astype(BF16)
    return hi, mid, lo


def _mod_kernel(c_ref, w_ref, b_ref, o_ref):
    c = c_ref[...]
    o_ref[...] = jnp.dot(_silu(c).astype(BF16), w_ref[...].astype(BF16),
                         preferred_element_type=F32) + b_ref[...]


def _mod_call(c_all, w_ada, b_ada):
    rows, n = c_all.shape[0], w_ada.shape[1]
    tn = 1024
    return pl.pallas_call(
        _mod_kernel,
        grid=(n // tn,),
        in_specs=[pl.BlockSpec((rows, D_MODEL), lambda j: (0, 0)),
                  pl.BlockSpec((D_MODEL, tn), lambda j: (0, j)),
                  pl.BlockSpec((1, tn), lambda j: (0, j))],
        out_specs=pl.BlockSpec((rows, tn), lambda j: (0, j)),
        out_shape=jax.ShapeDtypeStruct((rows, n), F32),
        compiler_params=_cparams(("arbitrary",)),
        name="ada_mod",
    )(c_all, w_ada, b_ada.reshape(1, n))


def _modulated_rms(x, nw, sc, sh):
    r = lax.rsqrt(jnp.mean(x * x, axis=-1, keepdims=True) + EPS)
    return (x * r) * nw * (1.0 + sc) + sh


def _prenorm_kernel(x_ref, nw_ref, sh_ref, sc_ref, o_ref):
    o_ref[...] = _modulated_rms(x_ref[...], nw_ref[...], sc_ref[...], sh_ref[...]).astype(BF16)


def _prenorm_call(x2, mod3, norm_w, seg_shift, seg_scale, tm, rows_per_batch):
    m = x2.shape[0]
    tpb = rows_per_batch // tm
    return pl.pallas_call(
        _prenorm_kernel,
        grid=(m // tm,),
        in_specs=[pl.BlockSpec((tm, D_MODEL), lambda i: (i, 0)),
                  pl.BlockSpec((1, D_MODEL), lambda i: (0, 0)),
                  pl.BlockSpec((None, 1, D_MODEL), lambda i: (i // tpb, 0, seg_shift)),
                  pl.BlockSpec((None, 1, D_MODEL), lambda i: (i // tpb, 0, seg_scale))],
        out_specs=pl.BlockSpec((tm, D_MODEL), lambda i: (i, 0)),
        out_shape=jax.ShapeDtypeStruct((m, D_MODEL), BF16),
        compiler_params=_cparams(("arbitrary",)),
        name="prenorm",
    )(x2, norm_w.reshape(1, D_MODEL), mod3, mod3)


def _mm_kernel(a_ref, w_ref, o_ref):
    o_ref[...] = jnp.dot(a_ref[...], w_ref[...], preferred_element_type=F32)


def _mm_call(a, w, tm, tn, name):
    m, k = a.shape
    n = w.shape[1]
    return pl.pallas_call(
        _mm_kernel,
        grid=(m // tm, n // tn),
        in_specs=[pl.BlockSpec((tm, k), lambda i, j: (i, 0)),
                  pl.BlockSpec((k, tn), lambda i, j: (0, j))],
        out_specs=pl.BlockSpec((tm, tn), lambda i, j: (i, j)),
        out_shape=jax.ShapeDtypeStruct((m, n), F32),
        compiler_params=_cparams(("arbitrary", "arbitrary")),
        name=name,
    )(a, w)


def _dtf_kernel(h_ref, w_ref, b_ref, o_ref):
    v = jnp.dot(h_ref[...], w_ref[...], preferred_element_type=F32) + b_ref[...]
    lane = lax.broadcasted_iota(jnp.int32, v.shape, 1)
    tail = jnp.log1p(jnp.exp(-jnp.abs(v)))
    softplus = jnp.maximum(v, 0.0) + tail
    log_sigmoid = jnp.minimum(v, 0.0) - tail
    o_ref[...] = jnp.where(lane < DT_LANES, softplus,
                           jnp.where(lane < LOGF_LANE0 + ATT_HEADS, log_sigmoid, 0.0))


def _dtf_call(h, w_small, b_small, tm):
    m = h.shape[0]
    return pl.pallas_call(
        _dtf_kernel,
        grid=(m // tm,),
        in_specs=[pl.BlockSpec((tm, D_MODEL), lambda i: (i, 0)),
                  pl.BlockSpec((D_MODEL, LANES), lambda i: (0, 0)),
                  pl.BlockSpec((1, LANES), lambda i: (0, 0))],
        out_specs=pl.BlockSpec((tm, LANES), lambda i: (i, 0)),
        out_shape=jax.ShapeDtypeStruct((m, LANES), F32),
        compiler_params=_cparams(("arbitrary",)),
        name="dt_logf_proj",
    )(h, w_small, b_small)


CUM_ROWS = 64


def _tri(n, lower):
    r = lax.broadcasted_iota(jnp.int32, (n, n), 0)
    c = lax.broadcasted_iota(jnp.int32, (n, n), 1)
    return (c <= r) if lower else (r <= c)


def _cumsum_rows(x, tril3):
    hi, mid, lo = _split3(x)
    return jnp.dot(tril3, jnp.concatenate([hi, mid, lo], axis=0), preferred_element_type=F32)


def _cumsum_kernel(x_ref, o_ref, *, n_chunks):
    tril = _tri(CUM_ROWS, True).astype(BF16)
    tril3 = jnp.concatenate([tril, tril, tril], axis=1)

    def body(c, carry):
        s = pl.multiple_of(c * CUM_ROWS, CUM_ROWS)
        y = _cumsum_rows(x_ref[pl.ds(s, CUM_ROWS), :], tril3) + carry
        o_ref[pl.ds(s, CUM_ROWS), :] = y
        return y[CUM_ROWS - 1:CUM_ROWS, :]

    lax.fori_loop(0, n_chunks, body, jnp.zeros((1, LANES), F32))


def _cumsum_call(x3):
    bn, length, _ = x3.shape
    return pl.pallas_call(
        functools.partial(_cumsum_kernel, n_chunks=length // CUM_ROWS),
        grid=(bn,),
        in_specs=[pl.BlockSpec((None, length, LANES), lambda b: (b, 0, 0))],
        out_specs=pl.BlockSpec((None, length, LANES), lambda b: (b, 0, 0)),
        out_shape=jax.ShapeDtypeStruct(x3.shape, F32),
        compiler_params=_cparams(("arbitrary",)),
        name="logf_cumsum",
    )(x3)


def _head_rms(blk, w_row):
    ms = jnp.mean(blk * blk, axis=-1, keepdims=True)
    return blk * lax.rsqrt(ms + EPS) * w_row


def _key_bias_columns(cum, head, rows):
    lane = lax.broadcasted_iota(jnp.int32, (rows, LANES), 1)
    c = jnp.broadcast_to(-cum[:, LOGF_LANE0 + head:LOGF_LANE0 + head + 1], (rows, LANES))
    hi, mid, lo = _split3(c)
    return jnp.where(lane == 0, hi.astype(F32),
                     jnp.where(lane == 1, mid.astype(F32),
                               jnp.where(lane == 2, lo.astype(F32), 0.0)))


def _q_kernel(h_ref, w_ref, nw_ref, q_ref):
    acc = jnp.dot(h_ref[...], w_ref[...], preferred_element_type=F32)
    for hh in range(ATT_HEADS):
        sl = slice(hh * ATT_HEAD_DIM, (hh + 1) * ATT_HEAD_DIM)
        q_ref[:, sl] = (_head_rms(acc[:, sl], nw_ref[...]) * ATT_SCALE).astype(BF16)


def _k_kernel(h_ref, w_ref, nw_ref, cum_ref, k_ref, kaug_ref):
    acc = jnp.dot(h_ref[...], w_ref[...], preferred_element_type=F32)
    rows = acc.shape[0]
    cum = cum_ref[...]
    for hh in range(ATT_HEADS):
        sl = slice(hh * ATT_HEAD_DIM, (hh + 1) * ATT_HEAD_DIM)
        kn = _head_rms(acc[:, sl], nw_ref[...])
        k_ref[:, sl] = kn
        kaug_ref[:, 2 * hh * LANES:(2 * hh + 1) * LANES] = kn.astype(BF16)
        kaug_ref[:, (2 * hh + 1) * LANES:(2 * hh + 2) * LANES] = _key_bias_columns(cum, hh, rows).astype(BF16)


def _kaug_kernel(k_ref, cum_ref, kaug_ref):
    rows = k_ref.shape[0]
    cum = cum_ref[...]
    for hh in range(ATT_HEADS):
        sl = slice(hh * ATT_HEAD_DIM, (hh + 1) * ATT_HEAD_DIM)
        kaug_ref[:, 2 * hh * LANES:(2 * hh + 1) * LANES] = k_ref[:, sl].astype(BF16)
        kaug_ref[:, (2 * hh + 1) * LANES:(2 * hh + 2) * LANES] = _key_bias_columns(cum, hh, rows).astype(BF16)


def _v_kernel(h_ref, w_ref, v_ref, vb_ref):
    acc = jnp.dot(h_ref[...], w_ref[...], preferred_element_type=F32)
    v_ref[...] = acc
    vb_ref[...] = acc.astype(BF16)


def _row_spec(tm, n):
    return pl.BlockSpec((tm, n), lambda i: (i, 0))


def _const_spec(r, n):
    return pl.BlockSpec((r, n), lambda i: (0, 0))


def _q_call(h, w, nw, tm):
    m = h.shape[0]
    return pl.pallas_call(
        _q_kernel, grid=(m // tm,),
        in_specs=[_row_spec(tm, D_MODEL), _const_spec(D_MODEL, ATT_WIDTH), _const_spec(1, ATT_HEAD_DIM)],
        out_specs=_row_spec(tm, ATT_WIDTH),
        out_shape=jax.ShapeDtypeStruct((m, ATT_WIDTH), BF16),
        compiler_params=_cparams(("arbitrary",)), name="q_proj",
    )(h, w, nw)


def _k_call(h, w, nw, cum2, tm):
    m = h.shape[0]
    return pl.pallas_call(
        _k_kernel, grid=(m // tm,),
        in_specs=[_row_spec(tm, D_MODEL), _const_spec(D_MODEL, ATT_WIDTH), _const_spec(1, ATT_HEAD_DIM),
                  _row_spec(tm, LANES)],
        out_specs=[_row_spec(tm, ATT_WIDTH), _row_spec(tm, 2 * ATT_WIDTH)],
        out_shape=[jax.ShapeDtypeStruct((m, ATT_WIDTH), F32), jax.ShapeDtypeStruct((m, 2 * ATT_WIDTH), BF16)],
        compiler_params=_cparams(("arbitrary",)), name="k_proj",
    )(h, w, nw, cum2)


def _kaug_call(k2, cum2, tm):
    m = k2.shape[0]
    return pl.pallas_call(
        _kaug_kernel, grid=(m // tm,),
        in_specs=[_row_spec(tm, ATT_WIDTH), _row_spec(tm, LANES)],
        out_specs=_row_spec(tm, 2 * ATT_WIDTH),
        out_shape=jax.ShapeDtypeStruct((m, 2 * ATT_WIDTH), BF16),
        compiler_params=_cparams(("arbitrary",)), name="k_cache_aug",
    )(k2, cum2)


def _v_call(h, w, tm):
    m = h.shape[0]
    return pl.pallas_call(
        _v_kernel, grid=(m // tm,),
        in_specs=[_row_spec(tm, D_MODEL), _const_spec(D_MODEL, ATT_WIDTH)],
        out_specs=[_row_spec(tm, ATT_WIDTH), _row_spec(tm, ATT_WIDTH)],
        out_shape=[jax.ShapeDtypeStruct((m, ATT_WIDTH), F32), jax.ShapeDtypeStruct((m, ATT_WIDTH), BF16)],
        compiler_params=_cparams(("arbitrary",)), name="v_proj",
    )(h, w)


def _attn_kernel(q_ref, k_ref, v_ref, o_ref, *, tq, tkp, n_past):
    i = pl.program_id(2)
    past = n_past * tkp
    lane = lax.broadcasted_iota(jnp.int32, (tq, LANES), 1)
    ones = jnp.where(lane < 3, 1.0, 0.0).astype(BF16)
    qa = jnp.concatenate([q_ref[...], ones], axis=1)

    def block(kb, vb, carry, masked):
        m, l, acc = carry
        st = lax.dot_general(kb, qa, (((1,), (1,)), ((), ())), preferred_element_type=F32)
        if masked:
            r = lax.broadcasted_iota(jnp.int32, st.shape, 0)
            c = lax.broadcasted_iota(jnp.int32, st.shape, 1)
            st = jnp.where(r <= c, st, NEG_BIG)
        m_new = jnp.maximum(m, jnp.max(st, axis=0, keepdims=True))
        alpha = jnp.exp(m - m_new)
        p = jnp.exp(st - m_new)
        l_new = alpha * l + jnp.sum(p, axis=0, keepdims=True)
        pv = lax.dot_general(vb, p.astype(BF16), (((0,), (0,)), ((), ())), preferred_element_type=F32)
        return m_new, l_new, alpha * acc + pv

    carry = (jnp.full((1, tq), NEG_BIG, F32), jnp.zeros((1, tq), F32), jnp.zeros((ATT_HEAD_DIM, tq), F32))
    for jp in range(n_past):
        carry = block(k_ref[jp * tkp:(jp + 1) * tkp, :], v_ref[jp * tkp:(jp + 1) * tkp, :], carry, False)

    def body(j, c):
        s = pl.multiple_of(past + j * tq, tq)
        return block(k_ref[pl.ds(s, tq), :], v_ref[pl.ds(s, tq), :], c, False)

    carry = lax.fori_loop(0, i, body, carry)
    sd = pl.multiple_of(past + i * tq, tq)
    _, l, acc = block(k_ref[pl.ds(sd, tq), :], v_ref[pl.ds(sd, tq), :], carry, True)
    o_ref[...] = (acc * (1.0 / l)).T.astype(BF16)


def _attn_call(q3, kaug3, vb3, tq, tkp, n_past):
    bn, t_q, _ = q3.shape
    t_k = kaug3.shape[1]
    return pl.pallas_call(
        functools.partial(_attn_kernel, tq=tq, tkp=tkp, n_past=n_past),
        grid=(bn, ATT_HEADS, t_q // tq),
        in_specs=[pl.BlockSpec((None, tq, ATT_HEAD_DIM), lambda b, h, i: (b, i, h)),
                  pl.BlockSpec((None, t_k, 2 * ATT_HEAD_DIM), lambda b, h, i: (b, 0, h)),
                  pl.BlockSpec((None, t_k, ATT_HEAD_DIM), lambda b, h, i: (b, 0, h))],
        out_specs=pl.BlockSpec((None, tq, ATT_HEAD_DIM), lambda b, h, i: (b, i, h)),
        out_shape=jax.ShapeDtypeStruct((bn, t_q, ATT_WIDTH), BF16),
        compiler_params=_cparams(("arbitrary", "arbitrary", "arbitrary")),
        name="fox_attention",
    )(q3, kaug3, vb3)


def _ssd_kernel(x_ref, b_ref, c_ref, z_ref, dt_ref, csx_ref, csb_ref, csc_ref, h0_ref,
                cwx_ref, cwb_ref, cwc_ref, cbx_ref, cbb_ref, cbc_ref, alog_ref, d_ref, nw_ref,
                y_ref, hout_ref,
                ext_x, ext_b, ext_c, xs, bs, cs, h_s, *, tb, q, n_chunks, n_tblocks):
    g = pl.program_id(1)
    t = pl.program_id(2)
    halo = SUBLANES_F32

    @pl.when(t == 0)
    def _():
        ext_x[0:halo, :] = csx_ref[...]
        ext_b[0:halo, :] = csb_ref[...]
        ext_c[0:halo, :] = csc_ref[...]
        h_s[...] = h0_ref[...]

    def conv(ext, raw_ref, cw_ref, cb_ref):
        ext[halo:halo + tb, :] = raw_ref[...]
        y = cb_ref[...]
        for j in range(SSD_CONV_W):
            off = halo - (SSD_CONV_W - 1) + j
            y = y + ext[off:off + tb, :] * cw_ref[j:j + 1, :]
        ext[0:halo, :] = ext[tb:tb + halo, :]
        return _silu(y)

    xs[...] = conv(ext_x, x_ref, cwx_ref, cbx_ref)
    bs[...] = conv(ext_b, b_ref, cwb_ref, cbb_ref).astype(BF16)
    cs[...] = conv(ext_c, c_ref, cwc_ref, cbc_ref).astype(BF16)

    lane_row = lax.broadcasted_iota(jnp.int32, (1, LANES), 1)
    a_row = jnp.where(lane_row < DT_LANES, -jnp.exp(alog_ref[...]), 0.0)
    shift = lax.rem(LANES - SSD_HEADS_PER_GROUP * g, LANES)
    tril_mask = _tri(q, True)
    tril = tril_mask.astype(BF16)
    tril3 = jnp.concatenate([tril, tril, tril], axis=1)
    lo_half = lax.broadcasted_iota(jnp.int32, (q, LANES), 1) < SSD_HEAD_DIM
    lo_half_row = lane_row < SSD_HEAD_DIM

    def chunk(ci, carry):
        s = pl.multiple_of(ci * q, q)
        dt_all = dt_ref[pl.ds(s, q), :]
        cum_all = _cumsum_rows(dt_all * a_row, tril3)
        cum_g = pltpu.roll(cum_all, shift, 1)
        dt_g = pltpu.roll(dt_all, shift, 1)
        cum_t = cum_g.T
        last = cum_g[q - 1:q, :]
        ecum = jnp.exp(cum_g)
        edec = jnp.exp(last - cum_g)
        etot = jnp.exp(last)

        xc = xs[pl.ds(s, q), :]
        bc = bs[pl.ds(s, q), :]
        cc = cs[pl.ds(s, q), :]
        cb = lax.dot_general(cc, bc, (((1,), (1,)), ((), ())), preferred_element_type=F32)
        h_t = h_s[...]
        y_inter = jnp.dot(cc, h_t.astype(BF16), preferred_element_type=F32)

        y_slabs, xd_slabs, etot_slabs = [], [], []
        for k in range(SSD_GROUP_W // LANES):
            h0i, h1i = 2 * k, 2 * k + 1

            def pair(arr):
                return jnp.where(lo_half, arr[:, h0i:h0i + 1], arr[:, h1i:h1i + 1])

            xdt = xc[:, k * LANES:(k + 1) * LANES] * pair(dt_g)
            y_slab = y_inter[:, k * LANES:(k + 1) * LANES] * pair(ecum)
            for hi_, sel in ((h0i, lo_half), (h1i, jnp.logical_not(lo_half))):
                seg = cum_g[:, hi_:hi_ + 1] - cum_t[hi_:hi_ + 1, :]
                lmat = jnp.exp(jnp.where(tril_mask, seg, -jnp.inf))
                mh = (cb * lmat).astype(BF16)
                rhs = jnp.where(sel, xdt, 0.0).astype(BF16)
                y_slab = y_slab + jnp.dot(mh, rhs, preferred_element_type=F32)
            y_slabs.append(y_slab)
            xd_slabs.append((xdt * pair(edec)).astype(BF16))
            etot_slabs.append(jnp.where(lo_half_row, etot[:, h0i:h0i + 1], etot[:, h1i:h1i + 1]))

        xd = jnp.concatenate(xd_slabs, axis=1)
        h_s[...] = h_t * jnp.concatenate(etot_slabs, axis=1) + lax.dot_general(
            bc, xd, (((0,), (0,)), ((), ())), preferred_element_type=F32)

        y = jnp.concatenate(y_slabs, axis=1) + xc * d_ref[...]
        y = y * _silu(z_ref[pl.ds(s, q), :])
        y = y * lax.rsqrt(jnp.mean(y * y, axis=-1, keepdims=True) + EPS) * nw_ref[...]
        y_ref[pl.ds(s, q), :] = y.astype(BF16)
        return carry

    lax.fori_loop(0, n_chunks, chunk, 0)

    @pl.when(t == n_tblocks - 1)
    def _():
        hout_ref[...] = h_s[...]


def _ssd_call(zx3, dtf3, cs8, h0t, conv_w, conv_b, a_log_row, d_row, norm_w_row, tb, q):
    bn, t_len, _ = zx3.shape
    n_tblocks = t_len // tb
    gw, st = SSD_GROUP_W, SSD_STATE
    zx_x0, zx_b0, zx_c0 = SSD_INNER // gw, 2 * SSD_INNER // st, 2 * SSD_INNER // st + SSD_GROUPS
    cv_b0, cv_c0 = SSD_INNER // st, SSD_INNER // st + SSD_GROUPS

    def rows(width, col0):
        return pl.BlockSpec((None, tb, width), lambda b, g, t: (b, t, col0 + g))

    def per_group(r, width, col0):
        return pl.BlockSpec((r, width), lambda b, g, t: (0, col0 + g))

    def tail(width, col0):
        return pl.BlockSpec((None, SUBLANES_F32, width), lambda b, g, t: (b, 0, col0 + g))

    in_specs = [
        rows(gw, zx_x0), rows(st, zx_b0), rows(st, zx_c0), rows(gw, 0),
        pl.BlockSpec((None, tb, LANES), lambda b, g, t: (b, t, 0)),
        tail(gw, 0), tail(st, cv_b0), tail(st, cv_c0),
        pl.BlockSpec((None, None, st, gw), lambda b, g, t: (b, g, 0, 0)),
        per_group(SSD_CONV_W, gw, 0), per_group(SSD_CONV_W, st, cv_b0), per_group(SSD_CONV_W, st, cv_c0),
        per_group(1, gw, 0), per_group(1, st, cv_b0), per_group(1, st, cv_c0),
        pl.BlockSpec((1, LANES), lambda b, g, t: (0, 0)),
        per_group(1, gw, 0), per_group(1, gw, 0),
    ]
    out_specs = [pl.BlockSpec((None, tb, gw), lambda b, g, t: (b, t, g)),
                 pl.BlockSpec((None, None, st, gw), lambda b, g, t: (b, g, 0, 0))]
    out_shape = [jax.ShapeDtypeStruct((bn, t_len, SSD_INNER), BF16),
                 jax.ShapeDtypeStruct((bn, SSD_GROUPS, st, gw), F32)]
    scratch = [pltpu.VMEM((tb + SUBLANES_F32, gw), F32), pltpu.VMEM((tb + SUBLANES_F32, st), F32),
               pltpu.VMEM((tb + SUBLANES_F32, st), F32),
               pltpu.VMEM((tb, gw), F32), pltpu.VMEM((tb, st), BF16), pltpu.VMEM((tb, st), BF16),
               pltpu.VMEM((st, gw), F32)]
    return pl.pallas_call(
        functools.partial(_ssd_kernel, tb=tb, q=q, n_chunks=tb // q, n_tblocks=n_tblocks),
        grid=(bn, SSD_GROUPS, n_tblocks),
        in_specs=in_specs, out_specs=out_specs, out_shape=out_shape, scratch_shapes=scratch,
        compiler_params=_cparams(("arbitrary", "arbitrary", "arbitrary")),
        name="ssd_scan",
    )(zx3, zx3, zx3, zx3, dtf3, cs8, cs8, cs8, h0t, conv_w, conv_w, conv_w, conv_b, conv_b, conv_b,
      a_log_row, d_row, norm_w_row)


def _merge_kernel(h_ref, ym_ref, yf_ref, wgm_ref, wgf_ref, wm_ref, wf_ref, o_ref):
    h = h_ref[...]
    gm = jax.nn.sigmoid(jnp.dot(h, wgm_ref[...], preferred_element_type=F32))
    gf = jax.nn.sigmoid(jnp.dot(h, wgf_ref[...], preferred_element_type=F32))
    a = jnp.dot(ym_ref[...], wm_ref[...], preferred_element_type=F32)
    b = jnp.dot(yf_ref[...], wf_ref[...], preferred_element_type=F32)
    o_ref[...] = (gm * a + gf * b).astype(BF16)


def _merge_call(h, ymn, yf, w_gates, w_pm, w_pf, tm, tn):
    m = h.shape[0]
    n_tiles = D_MODEL // tn
    return pl.pallas_call(
        _merge_kernel, grid=(m // tm, n_tiles),
        in_specs=[pl.BlockSpec((tm, D_MODEL), lambda i, j: (i, 0)),
                  pl.BlockSpec((tm, SSD_INNER), lambda i, j: (i, 0)),
                  pl.BlockSpec((tm, ATT_WIDTH), lambda i, j: (i, 0)),
                  pl.BlockSpec((D_MODEL, tn), lambda i, j: (0, j)),
                  pl.BlockSpec((D_MODEL, tn), lambda i, j: (0, n_tiles + j)),
                  pl.BlockSpec((SSD_INNER, tn), lambda i, j: (0, j)),
                  pl.BlockSpec((ATT_WIDTH, tn), lambda i, j: (0, j))],
        out_specs=pl.BlockSpec((tm, tn), lambda i, j: (i, j)),
        out_shape=jax.ShapeDtypeStruct((m, D_MODEL), BF16),
        compiler_params=_cparams(("arbitrary", "arbitrary")), name="gated_merge",
    )(h, ymn, yf, w_gates, w_gates, w_pm, w_pf)


def _outproj_kernel(mix_ref, w_ref, x_ref, g_ref, nw_ref, sh_ref, sc_ref, x1_ref, h2_ref):
    y = jnp.dot(mix_ref[...], w_ref[...], preferred_element_type=F32)
    x1 = x_ref[...] + g_ref[...] * y
    x1_ref[...] = x1
    h2_ref[...] = _modulated_rms(x1, nw_ref[...], sc_ref[...], sh_ref[...]).astype(BF16)


def _outproj_call(mix, w_out, x2, mod3, norm2_w, tm, rows_per_batch):
    m = x2.shape[0]
    tpb = rows_per_batch // tm

    def mod_seg(seg):
        return pl.BlockSpec((None, 1, D_MODEL), lambda i: (i // tpb, 0, seg))

    return pl.pallas_call(
        _outproj_kernel, grid=(m // tm,),
        in_specs=[_row_spec(tm, D_MODEL), _const_spec(D_MODEL, D_MODEL), _row_spec(tm, D_MODEL),
                  mod_seg(2), _const_spec(1, D_MODEL), mod_seg(3), mod_seg(4)],
        out_specs=[_row_spec(tm, D_MODEL), _row_spec(tm, D_MODEL)],
        out_shape=[jax.ShapeDtypeStruct((m, D_MODEL), F32), jax.ShapeDtypeStruct((m, D_MODEL), BF16)],
        compiler_params=_cparams(("arbitrary",)), name="out_proj",
    )(mix, w_out, x2, mod3, norm2_w.reshape(1, D_MODEL), mod3, mod3)


FFN_HALO = SUBLANES_BF16


def _ffn_kernel(h_ref, halo_ref, uia_ref, uib_ref, wa_ref, wb_ref, cwa_ref, cwb_ref, cba_ref, cbb_ref,
                wd_ref, x_ref, g_ref, o_ref, acc_ref, ext_ref, *, tm, tiles_per_seq, nf):
    i = pl.program_id(0)
    f = pl.program_id(1)
    first = (i % tiles_per_seq) == 0
    h = h_ref[...]
    hprev = halo_ref[...]

    def conv(w_ref, ui_ref, cw_ref, cb_ref):
        u = jnp.dot(h, w_ref[...], preferred_element_type=F32)
        uh = jnp.dot(hprev, w_ref[...], preferred_element_type=F32)
        ext_ref[0:FFN_HALO, :] = jnp.where(first, ui_ref[...], uh)
        ext_ref[FFN_HALO:FFN_HALO + tm, :] = u
        y = cb_ref[...]
        for j in range(FFN_CONV_W):
            off = FFN_HALO - (FFN_CONV_W - 1) + j
            y = y + ext_ref[off:off + tm, :] * cw_ref[j:j + 1, :]
        return y

    ua = conv(wa_ref, uia_ref, cwa_ref, cba_ref)
    ub = conv(wb_ref, uib_ref, cwb_ref, cbb_ref)
    part = jnp.dot((_silu(ua) * ub).astype(BF16), wd_ref[...], preferred_element_type=F32)

    @pl.when(f == 0)
    def _():
        acc_ref[...] = part

    @pl.when(f > 0)
    def _():
        acc_ref[...] += part

    @pl.when(f == nf - 1)
    def _():
        o_ref[...] = x_ref[...] + g_ref[...] * acc_ref[...]


def _ffn_call(h2, u_init, w_up_p, cw_p, cb_p, w_down_p, x1, mod3, tm, tf, rows_per_seq):
    m = h2.shape[0]
    nf = D_FF_PAD // tf
    tps = rows_per_seq // tm
    halo_blocks = tm // FFN_HALO
    return pl.pallas_call(
        functools.partial(_ffn_kernel, tm=tm, tiles_per_seq=tps, nf=nf),
        grid=(m // tm, nf),
        in_specs=[pl.BlockSpec((tm, D_MODEL), lambda i, f: (i, 0)),
                  pl.BlockSpec((FFN_HALO, D_MODEL), lambda i, f: (jnp.maximum(i * halo_blocks - 1, 0), 0)),
                  pl.BlockSpec((None, FFN_HALO, tf), lambda i, f: (i // tps, 0, f)),
                  pl.BlockSpec((None, FFN_HALO, tf), lambda i, f: (i // tps, 0, nf + f)),
                  pl.BlockSpec((D_MODEL, tf), lambda i, f: (0, f)),
                  pl.BlockSpec((D_MODEL, tf), lambda i, f: (0, nf + f)),
                  pl.BlockSpec((FFN_CONV_W, tf), lambda i, f: (0, f)),
                  pl.BlockSpec((FFN_CONV_W, tf), lambda i, f: (0, nf + f)),
                  pl.BlockSpec((1, tf), lambda i, f: (0, f)),
                  pl.BlockSpec((1, tf), lambda i, f: (0, nf + f)),
                  pl.BlockSpec((tf, D_MODEL), lambda i, f: (f, 0)),
                  pl.BlockSpec((tm, D_MODEL), lambda i, f: (i, 0)),
                  pl.BlockSpec((None, 1, D_MODEL), lambda i, f: (i // tps, 0, 5))],
        out_specs=pl.BlockSpec((tm, D_MODEL), lambda i, f: (i, 0)),
        out_shape=jax.ShapeDtypeStruct((m, D_MODEL), F32),
        scratch_shapes=[pltpu.VMEM((tm, D_MODEL), F32), pltpu.VMEM((tm + FFN_HALO, tf), F32)],
        compiler_params=_cparams(("arbitrary", "arbitrary")), name="conv_mlp",
    )(h2, h2, u_init, u_init, w_up_p, w_up_p, cw_p, cw_p, cb_p, cb_p, w_down_p, x1, mod3)


def _pad_ff(a, axis):
    lo, hi = jnp.split(a, 2, axis=axis)
    pad = [(0, 0)] * a.ndim
    pad[axis] = (0, D_FF_PAD - D_FF)
    return jnp.concatenate([jnp.pad(lo, pad), jnp.pad(hi, pad)], axis=axis)


def _prep_weights(p):
    w_in = p['w_in']
    o_z, o_xbc, o_dt = 0, SSD_INNER, SSD_INNER + SSD_CONV_DIM
    o_q = o_dt + SSD_HEADS
    o_k, o_v = o_q + ATT_WIDTH, o_q + 2 * ATT_WIDTH
    o_f = o_q + 3 * ATT_WIDTH
    o_g = o_f + ATT_HEADS
    pad_small = LANES - SSD_HEADS - ATT_HEADS
    w = {
        'zx': w_in[:, o_z:o_dt].astype(BF16),
        'small': jnp.concatenate([w_in[:, o_dt:o_q], w_in[:, o_f:o_g],
                                  jnp.zeros((D_MODEL, pad_small), F32)], axis=1).astype(BF16),
        'b_small': jnp.concatenate([p['m_dt_bias'], p['f_bias'], jnp.zeros((pad_small,), F32)]).reshape(1, LANES),
        'q': w_in[:, o_q:o_k].astype(BF16),
        'k': w_in[:, o_k:o_v].astype(BF16),
        'v': w_in[:, o_v:o_f].astype(BF16),
        'gates': w_in[:, o_g:o_g + 2 * D_MODEL].astype(BF16),
        'pm': p['w_proj_m'].astype(BF16),
        'pf': p['w_proj_f'].astype(BF16),
        'out': p['w_out'].astype(BF16),
        'up': _pad_ff(p['w_up'], 1).astype(BF16),
        'ffn_cw': _pad_ff(p['ffn_conv_w'], 1),
        'ffn_cb': _pad_ff(p['ffn_conv_b'].reshape(1, 2 * D_FF), 1),
        'down': jnp.pad(p['w_down'], ((0, D_FF_PAD - D_FF), (0, 0))).astype(BF16),
        'a_log': jnp.pad(p['m_a_log'], (0, LANES - SSD_HEADS)).reshape(1, LANES),
        'd_row': jnp.repeat(p['m_d'], SSD_HEAD_DIM).reshape(1, SSD_INNER),
    }
    return w


def _layer(x, mod, k_past, v_past, logf_past, ssm_h0, mconv_buf, ffn_buf, p, w, cfg):
    bn, t_len, _ = x.shape
    m = bn * t_len
    past = 0 if k_past is None else k_past.shape[1]
    mod3 = mod.reshape(bn, 1, N_MOD * D_MODEL)
    x2 = x.reshape(m, D_MODEL)

    h = _prenorm_call(x2, mod3, p['norm1_w'], 0, 1, cfg['tm_norm'], t_len)
    dtf = _dtf_call(h, w['small'], w['b_small'], cfg['tm_proj'])
    dtf3 = dtf.reshape(bn, t_len, LANES)
    logf = dtf3[:, :, LOGF_LANE0:LOGF_LANE0 + ATT_HEADS]

    if past:
        lf_past = jnp.pad(logf_past.astype(F32), ((0, 0), (0, 0), (LOGF_LANE0, LANES - LOGF_LANE0 - ATT_HEADS)))
        cum_all = _cumsum_call(jnp.concatenate([lf_past, dtf3], axis=1))
        cum_past, cum_new = cum_all[:, :past], cum_all[:, past:]
    else:
        cum_new = _cumsum_call(dtf3)

    zx = _mm_call(h, w['zx'], cfg['tm_zx'], 1024, "zx_proj")
    q = _q_call(h, w['q'], p['q_norm_w'].reshape(1, ATT_HEAD_DIM), cfg['tm_proj'])
    k_new, kaug = _k_call(h, w['k'], p['k_norm_w'].reshape(1, ATT_HEAD_DIM), cum_new.reshape(m, LANES), cfg['tm_proj'])
    v_new, vb = _v_call(h, w['v'], cfg['tm_proj'])

    q3 = q.reshape(bn, t_len, ATT_WIDTH)
    kaug3 = kaug.reshape(bn, t_len, 2 * ATT_WIDTH)
    vb3 = vb.reshape(bn, t_len, ATT_WIDTH)
    tq = cfg['tq']
    if past:
        kaug_past = _kaug_call(k_past.reshape(bn * past, ATT_WIDTH).astype(F32), cum_past.reshape(bn * past, LANES),
                               cfg['tm_proj']).reshape(bn, past, 2 * ATT_WIDTH)
        tpad = tq - t_len
        kaug3 = jnp.concatenate([kaug_past, jnp.pad(kaug3, ((0, 0), (0, tpad), (0, 0)))], axis=1)
        vb3 = jnp.concatenate([v_past.reshape(bn, past, ATT_WIDTH).astype(BF16),
                               jnp.pad(vb3, ((0, 0), (0, tpad), (0, 0)))], axis=1)
        q3 = jnp.pad(q3, ((0, 0), (0, tpad), (0, 0)))
        yf = _attn_call(q3, kaug3, vb3, tq, cfg['tkp'], past // cfg['tkp'])[:, :t_len]
    else:
        yf = _attn_call(q3, kaug3, vb3, tq, tq, 0)
    yf = yf.reshape(m, ATT_WIDTH)

    zx3 = zx.reshape(bn, t_len, SSD_INNER + SSD_CONV_DIM)
    cs8 = jnp.pad(mconv_buf.astype(F32), ((0, 0), (SUBLANES_F32 - (SSD_CONV_W - 1), 0), (0, 0)))
    hpg = SSD_HEADS_PER_GROUP
    h0t = ssm_h0.astype(F32).reshape(bn, SSD_GROUPS, hpg, SSD_HEAD_DIM, SSD_STATE)
    h0t = h0t.transpose(0, 1, 4, 2, 3).reshape(bn, SSD_GROUPS, SSD_STATE, SSD_GROUP_W)
    ymn, h_t = _ssd_call(zx3, dtf3, cs8, h0t, p['m_conv_w'], p['m_conv_b'].reshape(1, SSD_CONV_DIM),
                         w['a_log'], w['d_row'], p['m_norm_w'].reshape(1, SSD_INNER), cfg['tb'], cfg['q'])
    new_h = h_t.reshape(bn, SSD_GROUPS, SSD_STATE, hpg, SSD_HEAD_DIM).transpose(0, 1, 3, 4, 2)
    new_h = new_h.reshape(bn, SSD_HEADS, SSD_HEAD_DIM, SSD_STATE)
    new_mconv = zx3[:, t_len - (SSD_CONV_W - 1):, SSD_INNER:]

    mix = _merge_call(h, ymn.reshape(m, SSD_INNER), yf, w['gates'], w['pm'], w['pf'], cfg['tm_proj'], 512)
    x1, h2 = _outproj_call(mix, w['out'], x2, mod3, p['norm2_w'], cfg['tm_norm'], t_len)

    u_init = jnp.pad(_pad_ff(ffn_buf.astype(F32), 2), ((0, 0), (FFN_HALO - (FFN_CONV_W - 1), 0), (0, 0)))
    y = _ffn_call(h2, u_init, w['up'], w['ffn_cw'], w['ffn_cb'], w['down'], x1, mod3, cfg['tm_ffn'], 512, t_len)

    tail_rows = h2.reshape(bn, t_len, D_MODEL)[:, t_len - (FFN_CONV_W - 1):].reshape(bn * (FFN_CONV_W - 1), D_MODEL)
    n_tail = tail_rows.shape[0]
    tail_pad = -n_tail % SUBLANES_BF16
    u_tail = _mm_call(jnp.pad(tail_rows, ((0, tail_pad), (0, 0))), w['up'], n_tail + tail_pad, 1408, "ffn_state")
    u_tail = u_tail[:n_tail]
    new_ffn = jnp.concatenate([u_tail[:, :D_FF], u_tail[:, D_FF_PAD:D_FF_PAD + D_FF]], axis=1)
    new_ffn = new_ffn.reshape(bn, FFN_CONV_W - 1, 2 * D_FF)

    return (y.reshape(bn, t_len, D_MODEL), k_new.reshape(bn, t_len, ATT_HEADS, ATT_HEAD_DIM),
            v_new.reshape(bn, t_len, ATT_HEADS, ATT_HEAD_DIM), logf, new_h, new_mconv, new_ffn)


PROMPT_CFG = dict(tm_norm=512, tm_proj=512, tm_zx=1024, tq=1024, tkp=1024, tb=1024, q=128, tm_ffn=512)
SAMPLE_CFG = dict(tm_norm=64, tm_proj=512, tm_zx=512, tq=128, tkp=1024, tb=64, q=64, tm_ffn=64)


def kernel(x_prompt, x_sample, c_prompt, c_sample, cache_fox_k, cache_fox_v, cache_fox_logf, state_ssm, state_mamba_conv, state_ffn_conv, norm1_w, norm2_w, w_ada, b_ada, w_in, m_conv_w, m_conv_b, m_dt_bias, m_a_log, m_d, m_norm_w, f_bias, q_norm_w, k_norm_w, w_proj_m, w_proj_f, w_out, w_up, ffn_conv_w, ffn_conv_b, w_down):
    bp, bs = x_prompt.shape[0], x_sample.shape[0]
    depth = w_in.shape[0]
    xp, xs = x_prompt, x_sample
    c_all = jnp.concatenate([c_prompt, c_sample], axis=0)
    c_all = jnp.pad(c_all, ((0, -(bp + bs) % SUBLANES_BF16), (0, 0)))
    outs_p = [[] for _ in range(6)]
    outs_s = [[] for _ in range(6)]
    for l in range(depth):
        p = {
            'norm1_w': norm1_w[l], 'norm2_w': norm2_w[l], 'w_in': w_in[l], 'm_conv_w': m_conv_w[l],
            'm_conv_b': m_conv_b[l], 'm_dt_bias': m_dt_bias[l], 'm_a_log': m_a_log[l], 'm_d': m_d[l],
            'm_norm_w': m_norm_w[l], 'f_bias': f_bias[l], 'q_norm_w': q_norm_w[l], 'k_norm_w': k_norm_w[l],
            'w_proj_m': w_proj_m[l], 'w_proj_f': w_proj_f[l], 'w_out': w_out[l], 'w_up': w_up[l],
            'ffn_conv_w': ffn_conv_w[l], 'ffn_conv_b': ffn_conv_b[l], 'w_down': w_down[l],
        }
        w = _prep_weights(p)
        mod = _mod_call(c_all, w_ada[l], b_ada[l])
        dt = xp.dtype
        res_p = _layer(xp, mod[:bp], None, None, None,
                       jnp.zeros((bp, SSD_HEADS, SSD_HEAD_DIM, SSD_STATE), dt),
                       jnp.zeros((bp, SSD_CONV_W - 1, SSD_CONV_DIM), dt),
                       jnp.zeros((bp, FFN_CONV_W - 1, 2 * D_FF), dt), p, w, PROMPT_CFG)
        res_s = _layer(xs, mod[bp:bp + bs], cache_fox_k[l], cache_fox_v[l], cache_fox_logf[l], state_ssm[l],
                       state_mamba_conv[l], state_ffn_conv[l], p, w, SAMPLE_CFG)
        xp, xs = res_p[0], res_s[0]
        for i in range(6):
            outs_p[i].append(res_p[i + 1])
            outs_s[i].append(res_s[i + 1])
    k_p, v_p, lf_p, ssm_p, mc_p, fc_p = [jnp.stack(o, axis=0) for o in outs_p]
    k_s, v_s, lf_s, ssm_s, mc_s, fc_s = [jnp.stack(o, axis=0) for o in outs_s]
    return (xp, xs, k_p, v_p, lf_p, ssm_p, mc_p, fc_p, k_s, v_s, lf_s, ssm_s, mc_s, fc_s)
```

```python
import functools

import jax
import jax.numpy as jnp
from jax import lax
from jax.experimental import pallas as pl
from jax.experimental.pallas import tpu as pltpu

F32, BF16 = jnp.float32, jnp.bfloat16

D_MODEL = 2048
EPS = 1e-6
N_MOD = 6
SSD_INNER = 4096
SSD_HEADS = 64
SSD_HEAD_DIM = 64
SSD_GROUPS = 8
SSD_STATE = 128
SSD_CONV_W = 4
SSD_CONV_DIM = SSD_INNER + 2 * SSD_GROUPS * SSD_STATE
SSD_GROUP_W = SSD_INNER // SSD_GROUPS
SSD_HEADS_PER_GROUP = SSD_HEADS // SSD_GROUPS
ATT_HEADS = 16
ATT_HEAD_DIM = 128
ATT_WIDTH = ATT_HEADS * ATT_HEAD_DIM
ATT_SCALE = ATT_HEAD_DIM ** -0.5
D_FF = 5504
D_FF_PAD = 5632
FFN_CONV_W = 3
LANES = 128
SUBLANES_F32 = 8
SUBLANES_BF16 = 16
VMEM_LIMIT_BYTES = 56 * 1024 * 1024

DT_LANES = SSD_HEADS
LOGF_LANE0 = SSD_HEADS
NEG_BIG = -1e30
LOG2E = 1.4426950408889634
VT_ROWS = ATT_HEAD_DIM + SUBLANES_BF16
EXP_UNDERFLOW = 104.0


def _cparams(semantics):
    return pltpu.CompilerParams(dimension_semantics=semantics, vmem_limit_bytes=VMEM_LIMIT_BYTES)


def _silu(v):
    return v * jax.nn.sigmoid(v)


def _split3(v):
    hi = v.astype(BF16)
    r1 = v - hi.astype(F32)
    mid = r1.astype(BF16)
    lo = (r1 - mid.astype(F32)).astype(BF16)
    return hi, mid, lo


def _mod_kernel(c_ref, w_ref, b_ref, o_ref):
    c = c_ref[...]
    o_ref[...] = jnp.dot(_silu(c).astype(BF16), w_ref[...].astype(BF16),
                         preferred_element_type=F32) + b_ref[...]


def _mod_call(c_all, w_ada, b_ada):
    rows, n = c_all.shape[0], w_ada.shape[1]
    tn = 1024
    return pl.pallas_call(
        _mod_kernel,
        grid=(n // tn,),
        in_specs=[pl.BlockSpec((rows, D_MODEL), lambda j: (0, 0)),
                  pl.BlockSpec((D_MODEL, tn), lambda j: (0, j)),
                  pl.BlockSpec((1, tn), lambda j: (0, j))],
        out_specs=pl.BlockSpec((rows, tn), lambda j: (0, j)),
        out_shape=jax.ShapeDtypeStruct((rows, n), F32),
        compiler_params=_cparams(("arbitrary",)),
        name="ada_mod",
    )(c_all, w_ada, b_ada.reshape(1, n))


def _modulated_rms(x, nw, sc, sh):
    r = lax.rsqrt(jnp.mean(x * x, axis=-1, keepdims=True) + EPS)
    return (x * r) * nw * (1.0 + sc) + sh


def _prenorm_kernel(x_ref, nw_ref, sh_ref, sc_ref, o_ref):
    o_ref[...] = _modulated_rms(x_ref[...], nw_ref[...], sc_ref[...], sh_ref[...]).astype(BF16)


def _prenorm_call(x2, mod3, norm_w, seg_shift, seg_scale, tm, rows_per_batch):
    m = x2.shape[0]
    tpb = rows_per_batch // tm
    return pl.pallas_call(
        _prenorm_kernel,
        grid=(m // tm,),
        in_specs=[pl.BlockSpec((tm, D_MODEL), lambda i: (i, 0)),
                  pl.BlockSpec((1, D_MODEL), lambda i: (0, 0)),
                  pl.BlockSpec((None, 1, D_MODEL), lambda i: (i // tpb, 0, seg_shift)),
                  pl.BlockSpec((None, 1, D_MODEL), lambda i: (i // tpb, 0, seg_scale))],
        out_specs=pl.BlockSpec((tm, D_MODEL), lambda i: (i, 0)),
        out_shape=jax.ShapeDtypeStruct((m, D_MODEL), BF16),
        compiler_params=_cparams(("arbitrary",)),
        name="prenorm",
    )(x2, norm_w.reshape(1, D_MODEL), mod3, mod3)


def _mm_kernel(a_ref, w_ref, o_ref):
    o_ref[...] = jnp.dot(a_ref[...], w_ref[...], preferred_element_type=F32)


def _mm_call(a, w, tm, tn, name):
    m, k = a.shape
    n = w.shape[1]
    return pl.pallas_call(
        _mm_kernel,
        grid=(m // tm, n // tn),
        in_specs=[pl.BlockSpec((tm, k), lambda i, j: (i, 0)),
                  pl.BlockSpec((k, tn), lambda i, j: (0, j))],
        out_specs=pl.BlockSpec((tm, tn), lambda i, j: (i, j)),
        out_shape=jax.ShapeDtypeStruct((m, n), F32),
        compiler_params=_cparams(("arbitrary", "arbitrary")),
        name=name,
    )(a, w)


def _dtf_kernel(h_ref, w_ref, b_ref, o_ref):
    v = jnp.dot(h_ref[...], w_ref[...], preferred_element_type=F32) + b_ref[...]
    lane = lax.broadcasted_iota(jnp.int32, v.shape, 1)
    tail = jnp.log1p(jnp.exp(-jnp.abs(v)))
    softplus = jnp.maximum(v, 0.0) + tail
    log_sigmoid = jnp.minimum(v, 0.0) - tail
    o_ref[...] = jnp.where(lane < DT_LANES, softplus,
                           jnp.where(lane < LOGF_LANE0 + ATT_HEADS, log_sigmoid, 0.0))


def _dtf_call(h, w_small, b_small, tm):
    m = h.shape[0]
    return pl.pallas_call(
        _dtf_kernel,
        grid=(m // tm,),
        in_specs=[pl.BlockSpec((tm, D_MODEL), lambda i: (i, 0)),
                  pl.BlockSpec((D_MODEL, LANES), lambda i: (0, 0)),
                  pl.BlockSpec((1, LANES), lambda i: (0, 0))],
        out_specs=pl.BlockSpec((tm, LANES), lambda i: (i, 0)),
        out_shape=jax.ShapeDtypeStruct((m, LANES), F32),
        compiler_params=_cparams(("arbitrary",)),
        name="dt_logf_proj",
    )(h, w_small, b_small)


CUM_ROWS = 64


def _tri(n, lower):
    r = lax.broadcasted_iota(jnp.int32, (n, n), 0)
    c = lax.broadcasted_iota(jnp.int32, (n, n), 1)
    return (c <= r) if lower else (r <= c)


def _cumsum_rows(x, tril3):
    hi, mid, lo = _split3(x)
    return jnp.dot(tril3, jnp.concatenate([hi, mid, lo], axis=0), preferred_element_type=F32)


def _cumsum_kernel(x_ref, o_ref, *, n_chunks):
    tril = _tri(CUM_ROWS, True).astype(BF16)
    tril3 = jnp.concatenate([tril, tril, tril], axis=1)

    def body(c, carry):
        s = pl.multiple_of(c * CUM_ROWS, CUM_ROWS)
        y = _cumsum_rows(x_ref[pl.ds(s, CUM_ROWS), :], tril3) + carry
        o_ref[pl.ds(s, CUM_ROWS), :] = y
        return y[CUM_ROWS - 1:CUM_ROWS, :]

    lax.fori_loop(0, n_chunks, body, jnp.zeros((1, LANES), F32))


def _cumsum_call(x3):
    bn, length, _ = x3.shape
    return pl.pallas_call(
        functools.partial(_cumsum_kernel, n_chunks=length // CUM_ROWS),
        grid=(bn,),
        in_specs=[pl.BlockSpec((None, length, LANES), lambda b: (b, 0, 0))],
        out_specs=pl.BlockSpec((None, length, LANES), lambda b: (b, 0, 0)),
        out_shape=jax.ShapeDtypeStruct(x3.shape, F32),
        compiler_params=_cparams(("arbitrary",)),
        name="logf_cumsum",
    )(x3)


def _head_rms(blk, w_row):
    ms = jnp.mean(blk * blk, axis=-1, keepdims=True)
    return blk * lax.rsqrt(ms + EPS) * w_row


def _key_bias_columns(cum, head, rows):
    lane = lax.broadcasted_iota(jnp.int32, (rows, LANES), 1)
    c = jnp.broadcast_to(cum[:, LOGF_LANE0 + head:LOGF_LANE0 + head + 1] * (-LOG2E), (rows, LANES))
    hi, mid, lo = _split3(c)
    return jnp.where(lane == 0, hi.astype(F32),
                     jnp.where(lane == 1, mid.astype(F32),
                               jnp.where(lane == 2, lo.astype(F32), 0.0)))


def _q_kernel(h_ref, w_ref, nw_ref, q_ref):
    acc = jnp.dot(h_ref[...], w_ref[...], preferred_element_type=F32)
    for hh in range(ATT_HEADS):
        sl = slice(hh * ATT_HEAD_DIM, (hh + 1) * ATT_HEAD_DIM)
        q_ref[:, sl] = (_head_rms(acc[:, sl], nw_ref[...]) * (ATT_SCALE * LOG2E)).astype(BF16)


def _k_kernel(h_ref, w_ref, nw_ref, cum_ref, k_ref, kaug_ref):
    acc = jnp.dot(h_ref[...], w_ref[...], preferred_element_type=F32)
    rows = acc.shape[0]
    cum = cum_ref[...]
    for hh in range(ATT_HEADS):
        sl = slice(hh * ATT_HEAD_DIM, (hh + 1) * ATT_HEAD_DIM)
        kn = _head_rms(acc[:, sl], nw_ref[...])
        k_ref[:, sl] = kn
        kaug_ref[:, 2 * hh * LANES:(2 * hh + 1) * LANES] = kn.astype(BF16)
        kaug_ref[:, (2 * hh + 1) * LANES:(2 * hh + 2) * LANES] = _key_bias_columns(cum, hh, rows).astype(BF16)


def _kaug_kernel(k_ref, cum_ref, kaug_ref):
    rows = k_ref.shape[0]
    cum = cum_ref[...]
    for hh in range(ATT_HEADS):
        sl = slice(hh * ATT_HEAD_DIM, (hh + 1) * ATT_HEAD_DIM)
        kaug_ref[:, 2 * hh * LANES:(2 * hh + 1) * LANES] = k_ref[:, sl].astype(BF16)
        kaug_ref[:, (2 * hh + 1) * LANES:(2 * hh + 2) * LANES] = _key_bias_columns(cum, hh, rows).astype(BF16)


def _v_kernel(h_ref, w_ref, v_ref, vt_ref):
    acc = jnp.dot(h_ref[...], w_ref[...], preferred_element_type=F32)
    tm = acc.shape[0]
    v_ref[...] = acc
    row = lax.broadcasted_iota(jnp.int32, (VT_ROWS - ATT_HEAD_DIM, tm), 0)
    ones_rows = jnp.where(row == 0, 1.0, 0.0).astype(BF16)
    for hh in range(ATT_HEADS):
        vt_ref[hh, 0:ATT_HEAD_DIM, :] = acc[:, hh * ATT_HEAD_DIM:(hh + 1) * ATT_HEAD_DIM].T.astype(BF16)
        vt_ref[hh, ATT_HEAD_DIM:VT_ROWS, :] = ones_rows


def _row_spec(tm, n):
    return pl.BlockSpec((tm, n), lambda i: (i, 0))


def _const_spec(r, n):
    return pl.BlockSpec((r, n), lambda i: (0, 0))


def _q_call(h, w, nw, tm):
    m = h.shape[0]
    return pl.pallas_call(
        _q_kernel, grid=(m // tm,),
        in_specs=[_row_spec(tm, D_MODEL), _const_spec(D_MODEL, ATT_WIDTH), _const_spec(1, ATT_HEAD_DIM)],
        out_specs=_row_spec(tm, ATT_WIDTH),
        out_shape=jax.ShapeDtypeStruct((m, ATT_WIDTH), BF16),
        compiler_params=_cparams(("arbitrary",)), name="q_proj",
    )(h, w, nw)


def _k_call(h, w, nw, cum2, tm):
    m = h.shape[0]
    return pl.pallas_call(
        _k_kernel, grid=(m // tm,),
        in_specs=[_row_spec(tm, D_MODEL), _const_spec(D_MODEL, ATT_WIDTH), _const_spec(1, ATT_HEAD_DIM),
                  _row_spec(tm, LANES)],
        out_specs=[_row_spec(tm, ATT_WIDTH), _row_spec(tm, 2 * ATT_WIDTH)],
        out_shape=[jax.ShapeDtypeStruct((m, ATT_WIDTH), F32), jax.ShapeDtypeStruct((m, 2 * ATT_WIDTH), BF16)],
        compiler_params=_cparams(("arbitrary",)), name="k_proj",
    )(h, w, nw, cum2)


def _kaug_call(k2, cum2, tm):
    m = k2.shape[0]
    return pl.pallas_call(
        _kaug_kernel, grid=(m // tm,),
        in_specs=[_row_spec(tm, ATT_WIDTH), _row_spec(tm, LANES)],
        out_specs=_row_spec(tm, 2 * ATT_WIDTH),
        out_shape=jax.ShapeDtypeStruct((m, 2 * ATT_WIDTH), BF16),
        compiler_params=_cparams(("arbitrary",)), name="k_cache_aug",
    )(k2, cum2)


def _v_call(h, w, tm, bn, t_len):
    m = h.shape[0]
    tpb = t_len // tm
    return pl.pallas_call(
        _v_kernel, grid=(m // tm,),
        in_specs=[_row_spec(tm, D_MODEL), _const_spec(D_MODEL, ATT_WIDTH)],
        out_specs=[_row_spec(tm, ATT_WIDTH),
                   pl.BlockSpec((None, ATT_HEADS, VT_ROWS, tm), lambda i: (i // tpb, 0, 0, i % tpb))],
        out_shape=[jax.ShapeDtypeStruct((m, ATT_WIDTH), F32),
                   jax.ShapeDtypeStruct((bn, ATT_HEADS, VT_ROWS, t_len), BF16)],
        compiler_params=_cparams(("arbitrary",)), name="v_proj",
    )(h, w)


def _attn_kernel(js_ref, q_ref, k_ref, vt_ref, o_ref, *scratch, tq, tk, sw, past):
    b, h, i = pl.program_id(0), pl.program_id(1), pl.program_id(2)
    nq = pl.num_programs(2)
    n_strips = tq // sw
    acc_refs, s_refs = scratch[:n_strips], scratch[n_strips:]
    lane = lax.broadcasted_iota(jnp.int32, (tq, LANES), 1)
    ones = jnp.where(lane < 3, 1.0, 0.0).astype(BF16)
    qa = jnp.concatenate([q_ref[...], ones], axis=1)
    qas = [qa[c * sw:(c + 1) * sw, :] for c in range(n_strips)]
    for acc_ref in acc_refs:
        acc_ref[...] = jnp.zeros(acc_ref.shape, F32)
    diag_r = lax.broadcasted_iota(jnp.int32, (sw, sw), 0)
    diag_c = lax.broadcasted_iota(jnp.int32, (sw, sw), 1)

    def strip(c, kb, vt_blk, m_c, n, masked):
        s_ref, acc_ref = s_refs[c], acc_refs[c]
        st = lax.dot_general(kb, qas[c], (((1,), (1,)), ((), ())), preferred_element_type=F32)
        if masked:
            tail = jnp.where(diag_r <= diag_c, st[n - sw:, :], NEG_BIG)
            mx = jnp.max(tail, axis=0, keepdims=True)
            if n > sw:
                s_ref[0:n - sw, :] = st[:n - sw, :]
                mx = jnp.maximum(mx, jnp.max(st[:n - sw, :], axis=0, keepdims=True))
            s_ref[n - sw:n, :] = tail
        else:
            s_ref[0:n, :] = st
            mx = jnp.max(st, axis=0, keepdims=True)
        m_new = jnp.maximum(m_c, mx)
        alpha = jnp.exp2(m_c - m_new)
        p = jnp.exp2(s_ref[0:n, :] - m_new).astype(BF16)
        pv = jnp.dot(vt_blk, p, preferred_element_type=F32)
        acc_ref[...] = alpha * acc_ref[...] + pv
        return m_new

    def body(j, ms):
        s = pl.multiple_of(j * tk, tk)
        kb = k_ref[pl.ds(s, tk), :]
        vt_blk = vt_ref[:, pl.ds(s, tk)]
        return tuple(strip(c, kb, vt_blk, ms[c], tk, False) for c in range(n_strips))

    ms = tuple(jnp.full((1, sw), NEG_BIG, F32) for _ in range(n_strips))
    j_start = js_ref[(b * ATT_HEADS + h) * nq + i]
    ms = lax.fori_loop(j_start, past // tk + i * (tq // tk), body, ms)

    d0 = pl.multiple_of(past + i * tq, tq)
    for c in range(n_strips):
        n = (c + 1) * sw
        strip(c, k_ref[pl.ds(d0, n), :], vt_ref[:, pl.ds(d0, n)], ms[c], n, True)

    for c in range(n_strips):
        acc = acc_refs[c][...]
        o = acc[0:ATT_HEAD_DIM, :] * (1.0 / acc[ATT_HEAD_DIM:ATT_HEAD_DIM + 1, :])
        o_ref[c * sw:(c + 1) * sw, :] = o.T.astype(BF16)


def _attn_block_starts(cum_all, q_norm_w, k_norm_w, tq, tk, past, nq):
    bn = cum_all.shape[0]
    bound = ATT_HEAD_DIM * jnp.max(jnp.abs(q_norm_w)) * jnp.max(jnp.abs(k_norm_w)) * ATT_SCALE
    thr = EXP_UNDERFLOW + 2.0 * bound
    c = cum_all[:, :, LOGF_LANE0:LOGF_LANE0 + ATT_HEADS]
    n_off = past // tk + jnp.arange(nq) * (tq // tk)
    nb = past // tk + (nq - 1) * (tq // tk)
    if nb == 0:
        return jnp.zeros((bn * ATT_HEADS * nq,), jnp.int32)
    t_real = c.shape[1] - past
    cq = jnp.pad(c[:, past:], ((0, 0), (0, nq * tq - t_real), (0, 0)), constant_values=-jnp.inf)
    cq_max = jnp.max(cq.reshape(bn, nq, tq, ATT_HEADS), axis=2)
    ck_min = jnp.min(c[:, :nb * tk].reshape(bn, nb, tk, ATT_HEADS), axis=2)
    gap = cq_max[:, :, None, :] - ck_min[:, None, :, :]
    jj = jnp.arange(nb)[None, None, :, None]
    cand = jnp.where((gap >= -thr) & (jj < n_off[None, :, None, None]), jj, n_off[None, :, None, None])
    js = jnp.min(cand, axis=2)
    return js.transpose(0, 2, 1).reshape(-1).astype(jnp.int32)


def _attn_call(js, q3, kaug3, vt4, tq, tk, sw, past):
    bn, t_q, _ = q3.shape
    t_k = kaug3.shape[1]
    n_strips = tq // sw
    grid_spec = pltpu.PrefetchScalarGridSpec(
        num_scalar_prefetch=1,
        grid=(bn, ATT_HEADS, t_q // tq),
        in_specs=[pl.BlockSpec((None, tq, ATT_HEAD_DIM), lambda b, h, i, js_ref: (b, i, h)),
                  pl.BlockSpec((None, t_k, 2 * ATT_HEAD_DIM), lambda b, h, i, js_ref: (b, 0, h)),
                  pl.BlockSpec((None, None, VT_ROWS, t_k), lambda b, h, i, js_ref: (b, h, 0, 0))],
        out_specs=pl.BlockSpec((None, tq, ATT_HEAD_DIM), lambda b, h, i, js_ref: (b, i, h)),
        scratch_shapes=([pltpu.VMEM((VT_ROWS, sw), F32)] * n_strips
                        + [pltpu.VMEM((max(tk, (c + 1) * sw), sw), F32) for c in range(n_strips)]))
    return pl.pallas_call(
        functools.partial(_attn_kernel, tq=tq, tk=tk, sw=sw, past=past),
        grid_spec=grid_spec,
        out_shape=jax.ShapeDtypeStruct((bn, t_q, ATT_WIDTH), BF16),
        compiler_params=_cparams(("arbitrary", "arbitrary", "arbitrary")),
        name="fox_attention",
    )(js, q3, kaug3, vt4)


def _ssd_kernel(x_ref, b_ref, c_ref, z_ref, dt_ref, csx_ref, csb_ref, csc_ref, h0_ref,
                cwx_ref, cwb_ref, cwc_ref, cbx_ref, cbb_ref, cbc_ref, alog_ref, d_ref, nw_ref,
                y_ref, hout_ref,
                ext_x, ext_b, ext_c, xs, bs, cs, h_s, *, tb, q, n_chunks, n_tblocks):
    g = pl.program_id(1)
    t = pl.program_id(2)
    halo = SUBLANES_F32

    @pl.when(t == 0)
    def _():
        ext_x[0:halo, :] = csx_ref[...]
        ext_b[0:halo, :] = csb_ref[...]
        ext_c[0:halo, :] = csc_ref[...]
        h_s[...] = h0_ref[...]

    def conv(ext, raw_ref, cw_ref, cb_ref):
        ext[halo:halo + tb, :] = raw_ref[...]
        e = ext[...]
        y = cb_ref[...]
        for j in range(SSD_CONV_W):
            back = SSD_CONV_W - 1 - j
            win = e if back == 0 else pltpu.roll(e, back, 0)
            y = y + win[halo:halo + tb, :] * cw_ref[j:j + 1, :]
        ext[0:halo, :] = e[tb:tb + halo, :]
        return _silu(y)

    xs[...] = conv(ext_x, x_ref, cwx_ref, cbx_ref)
    bs[...] = conv(ext_b, b_ref, cwb_ref, cbb_ref).astype(BF16)
    cs[...] = conv(ext_c, c_ref, cwc_ref, cbc_ref).astype(BF16)

    lane_row = lax.broadcasted_iota(jnp.int32, (1, LANES), 1)
    a_row = jnp.where(lane_row < DT_LANES, -jnp.exp(alog_ref[...]), 0.0)
    shift = lax.rem(LANES - SSD_HEADS_PER_GROUP * g, LANES)
    tril_mask = _tri(q, True)
    tril = tril_mask.astype(BF16)
    tril3 = jnp.concatenate([tril, tril, tril], axis=1)
    lo_half = lax.broadcasted_iota(jnp.int32, (q, LANES), 1) < SSD_HEAD_DIM

    def chunk(ci, carry):
        s = pl.multiple_of(ci * q, q)
        dt_all = dt_ref[pl.ds(s, q), :]
        cum_all = _cumsum_rows(dt_all * a_row, tril3)
        cum_g = pltpu.roll(cum_all, shift, 1)
        dt_g = pltpu.roll(dt_all, shift, 1)
        cum_t = cum_g.T

        xc = xs[pl.ds(s, q), :]
        bc = bs[pl.ds(s, q), :]
        cc = cs[pl.ds(s, q), :]
        cb = lax.dot_general(cc, bc, (((1,), (1,)), ((), ())), preferred_element_type=F32)
        h_t = h_s[...]
        y_inter = jnp.dot(cc, h_t.astype(BF16), preferred_element_type=F32)

        y_slabs, xd_slabs, etot_slabs = [], [], []
        for k in range(SSD_GROUP_W // LANES):
            h0i, h1i = 2 * k, 2 * k + 1
            cum_b = {hh: jnp.broadcast_to(cum_g[:, hh:hh + 1], (q, LANES)) for hh in (h0i, h1i)}
            dt_b = {hh: jnp.broadcast_to(dt_g[:, hh:hh + 1], (q, LANES)) for hh in (h0i, h1i)}
            cum_pair = jnp.where(lo_half, cum_b[h0i], cum_b[h1i])
            last_pair = cum_pair[q - 1:q, :]
            xdt = xc[:, k * LANES:(k + 1) * LANES] * jnp.where(lo_half, dt_b[h0i], dt_b[h1i])
            y_slab = y_inter[:, k * LANES:(k + 1) * LANES] * jnp.exp(cum_pair)
            for hi_, sel in ((h0i, lo_half), (h1i, jnp.logical_not(lo_half))):
                seg = cum_b[hi_][:, 0:q] - cum_t[hi_:hi_ + 1, :]
                lmat = jnp.exp(jnp.where(tril_mask, seg, -jnp.inf))
                mh = (cb * lmat).astype(BF16)
                rhs = jnp.where(sel, xdt, 0.0).astype(BF16)
                y_slab = y_slab + jnp.dot(mh, rhs, preferred_element_type=F32)
            y_slabs.append(y_slab)
            xd_slabs.append((xdt * jnp.exp(last_pair - cum_pair)).astype(BF16))
            etot_slabs.append(jnp.exp(last_pair))

        xd = jnp.concatenate(xd_slabs, axis=1)
        h_s[...] = h_t * jnp.concatenate(etot_slabs, axis=1) + lax.dot_general(
            bc, xd, (((0,), (0,)), ((), ())), preferred_element_type=F32)

        y = jnp.concatenate(y_slabs, axis=1) + xc * d_ref[...]
        y = y * _silu(z_ref[pl.ds(s, q), :])
        y = y * lax.rsqrt(jnp.mean(y * y, axis=-1, keepdims=True) + EPS) * nw_ref[...]
        y_ref[pl.ds(s, q), :] = y.astype(BF16)
        return carry

    lax.fori_loop(0, n_chunks, chunk, 0, unroll=min(2, n_chunks))

    @pl.when(t == n_tblocks - 1)
    def _():
        hout_ref[...] = h_s[...]


def _ssd_call(zx3, dtf3, cs8, h0t, conv_w, conv_b, a_log_row, d_row, norm_w_row, tb, q):
    bn, t_len, _ = zx3.shape
    n_tblocks = t_len // tb
    gw, st = SSD_GROUP_W, SSD_STATE
    zx_x0, zx_b0, zx_c0 = SSD_INNER // gw, 2 * SSD_INNER // st, 2 * SSD_INNER // st + SSD_GROUPS
    cv_b0, cv_c0 = SSD_INNER // st, SSD_INNER // st + SSD_GROUPS

    def rows(width, col0):
        return pl.BlockSpec((None, tb, width), lambda b, g, t: (b, t, col0 + g))

    def per_group(r, width, col0):
        return pl.BlockSpec((r, width), lambda b, g, t: (0, col0 + g))

    def tail(width, col0):
        return pl.BlockSpec((None, SUBLANES_F32, width), lambda b, g, t: (b, 0, col0 + g))

    in_specs = [
        rows(gw, zx_x0), rows(st, zx_b0), rows(st, zx_c0), rows(gw, 0),
        pl.BlockSpec((None, tb, LANES), lambda b, g, t: (b, t, 0)),
        tail(gw, 0), tail(st, cv_b0), tail(st, cv_c0),
        pl.BlockSpec((None, None, st, gw), lambda b, g, t: (b, g, 0, 0)),
        per_group(SSD_CONV_W, gw, 0), per_group(SSD_CONV_W, st, cv_b0), per_group(SSD_CONV_W, st, cv_c0),
        per_group(1, gw, 0), per_group(1, st, cv_b0), per_group(1, st, cv_c0),
        pl.BlockSpec((1, LANES), lambda b, g, t: (0, 0)),
        per_group(1, gw, 0), per_group(1, gw, 0),
    ]
    out_specs = [pl.BlockSpec((None, tb, gw), lambda b, g, t: (b, t, g)),
                 pl.BlockSpec((None, None, st, gw), lambda b, g, t: (b, g, 0, 0))]
    out_shape = [jax.ShapeDtypeStruct((bn, t_len, SSD_INNER), BF16),
                 jax.ShapeDtypeStruct((bn, SSD_GROUPS, st, gw), F32)]
    scratch = [pltpu.VMEM((tb + SUBLANES_F32, gw), F32), pltpu.VMEM((tb + SUBLANES_F32, st), F32),
               pltpu.VMEM((tb + SUBLANES_F32, st), F32),
               pltpu.VMEM((tb, gw), F32), pltpu.VMEM((tb, st), BF16), pltpu.VMEM((tb, st), BF16),
               pltpu.VMEM((st, gw), F32)]
    return pl.pallas_call(
        functools.partial(_ssd_kernel, tb=tb, q=q, n_chunks=tb // q, n_tblocks=n_tblocks),
        grid=(bn, SSD_GROUPS, n_tblocks),
        in_specs=in_specs, out_specs=out_specs, out_shape=out_shape, scratch_shapes=scratch,
        compiler_params=_cparams(("arbitrary", "arbitrary", "arbitrary")),
        name="ssd_scan",
    )(zx3, zx3, zx3, zx3, dtf3, cs8, cs8, cs8, h0t, conv_w, conv_w, conv_w, conv_b, conv_b, conv_b,
      a_log_row, d_row, norm_w_row)


def _merge_kernel(h_ref, ym_ref, yf_ref, wgm_ref, wgf_ref, wm_ref, wf_ref, o_ref):
    h = h_ref[...]
    gm = jax.nn.sigmoid(jnp.dot(h, wgm_ref[...], preferred_element_type=F32))
    gf = jax.nn.sigmoid(jnp.dot(h, wgf_ref[...], preferred_element_type=F32))
    a = jnp.dot(ym_ref[...], wm_ref[...], preferred_element_type=F32)
    b = jnp.dot(yf_ref[...], wf_ref[...], preferred_element_type=F32)
    o_ref[...] = (gm * a + gf * b).astype(BF16)


def _merge_call(h, ymn, yf, w_gates, w_pm, w_pf, tm, tn):
    m = h.shape[0]
    n_tiles = D_MODEL // tn
    return pl.pallas_call(
        _merge_kernel, grid=(m // tm, n_tiles),
        in_specs=[pl.BlockSpec((tm, D_MODEL), lambda i, j: (i, 0)),
                  pl.BlockSpec((tm, SSD_INNER), lambda i, j: (i, 0)),
                  pl.BlockSpec((tm, ATT_WIDTH), lambda i, j: (i, 0)),
                  pl.BlockSpec((D_MODEL, tn), lambda i, j: (0, j)),
                  pl.BlockSpec((D_MODEL, tn), lambda i, j: (0, n_tiles + j)),
                  pl.BlockSpec((SSD_INNER, tn), lambda i, j: (0, j)),
                  pl.BlockSpec((ATT_WIDTH, tn), lambda i, j: (0, j))],
        out_specs=pl.BlockSpec((tm, tn), lambda i, j: (i, j)),
        out_shape=jax.ShapeDtypeStruct((m, D_MODEL), BF16),
        compiler_params=_cparams(("arbitrary", "arbitrary")), name="gated_merge",
    )(h, ymn, yf, w_gates, w_gates, w_pm, w_pf)


def _outproj_kernel(mix_ref, w_ref, x_ref, g_ref, nw_ref, sh_ref, sc_ref, x1_ref, h2_ref):
    y = jnp.dot(mix_ref[...], w_ref[...], preferred_element_type=F32)
    x1 = x_ref[...] + g_ref[...] * y
    x1_ref[...] = x1
    h2_ref[...] = _modulated_rms(x1, nw_ref[...], sc_ref[...], sh_ref[...]).astype(BF16)


def _outproj_call(mix, w_out, x2, mod3, norm2_w, tm, rows_per_batch):
    m = x2.shape[0]
    tpb = rows_per_batch // tm

    def mod_seg(seg):
        return pl.BlockSpec((None, 1, D_MODEL), lambda i: (i // tpb, 0, seg))

    return pl.pallas_call(
        _outproj_kernel, grid=(m // tm,),
        in_specs=[_row_spec(tm, D_MODEL), _const_spec(D_MODEL, D_MODEL), _row_spec(tm, D_MODEL),
                  mod_seg(2), _const_spec(1, D_MODEL), mod_seg(3), mod_seg(4)],
        out_specs=[_row_spec(tm, D_MODEL), _row_spec(tm, D_MODEL)],
        out_shape=[jax.ShapeDtypeStruct((m, D_MODEL), F32), jax.ShapeDtypeStruct((m, D_MODEL), BF16)],
        compiler_params=_cparams(("arbitrary",)), name="out_proj",
    )(mix, w_out, x2, mod3, norm2_w.reshape(1, D_MODEL), mod3, mod3)


FFN_HALO = SUBLANES_BF16


def _ffn_kernel(h_ref, halo_ref, uia_ref, uib_ref, wa_ref, wb_ref, cwa_ref, cwb_ref, cba_ref, cbb_ref,
                wd_ref, x_ref, g_ref, o_ref, acc_ref, gact_ref, *, tm, tiles_per_seq, nf):
    i = pl.program_id(0)
    f = pl.program_id(1)
    first = (i % tiles_per_seq) == 0

    def up_stage():
        h = h_ref[...]
        hprev = halo_ref[...]

        def conv(w_ref, ui_ref, cw_ref, cb_ref):
            u = jnp.dot(h, w_ref[...], preferred_element_type=F32)
            uh = jnp.dot(hprev, w_ref[...], preferred_element_type=F32)
            e = jnp.concatenate([jnp.where(first, ui_ref[...], uh), u], axis=0)
            y = cb_ref[...]
            for j in range(FFN_CONV_W):
                back = FFN_CONV_W - 1 - j
                win = e if back == 0 else pltpu.roll(e, back, 0)
                y = y + win[FFN_HALO:FFN_HALO + tm, :] * cw_ref[j:j + 1, :]
            return y

        ua = conv(wa_ref, uia_ref, cwa_ref, cba_ref)
        ub = conv(wb_ref, uib_ref, cwb_ref, cbb_ref)
        return (_silu(ua) * ub).astype(BF16)

    def down_stage(slot):
        return jnp.dot(gact_ref[slot], wd_ref[...], preferred_element_type=F32)

    @pl.when(f == 0)
    def _():
        gact_ref[0] = up_stage()
        acc_ref[...] = jnp.zeros(acc_ref.shape, F32)

    @pl.when(jnp.logical_and(f > 0, f < nf))
    def _():
        slot = f % 2
        g_new = up_stage()
        acc_ref[...] += down_stage(1 - slot)
        gact_ref[slot] = g_new

    @pl.when(f == nf)
    def _():
        o_ref[...] = x_ref[...] + g_ref[...] * (acc_ref[...] + down_stage((nf - 1) % 2))


def _ffn_call(h2, u_init, w_up_p, cw_p, cb_p, w_down_p, x1, mod3, tm, tf, rows_per_seq):
    m = h2.shape[0]
    nf = D_FF_PAD // tf
    tps = rows_per_seq // tm
    halo_blocks = tm // FFN_HALO

    def up_f(f):
        return jnp.minimum(f, nf - 1)

    return pl.pallas_call(
        functools.partial(_ffn_kernel, tm=tm, tiles_per_seq=tps, nf=nf),
        grid=(m // tm, nf + 1),
        in_specs=[pl.BlockSpec((tm, D_MODEL), lambda i, f: (i, 0)),
                  pl.BlockSpec((FFN_HALO, D_MODEL), lambda i, f: (jnp.maximum(i * halo_blocks - 1, 0), 0)),
                  pl.BlockSpec((None, FFN_HALO, tf), lambda i, f: (i // tps, 0, up_f(f))),
                  pl.BlockSpec((None, FFN_HALO, tf), lambda i, f: (i // tps, 0, nf + up_f(f))),
                  pl.BlockSpec((D_MODEL, tf), lambda i, f: (0, up_f(f))),
                  pl.BlockSpec((D_MODEL, tf), lambda i, f: (0, nf + up_f(f))),
                  pl.BlockSpec((FFN_CONV_W, tf), lambda i, f: (0, up_f(f))),
                  pl.BlockSpec((FFN_CONV_W, tf), lambda i, f: (0, nf + up_f(f))),
                  pl.BlockSpec((1, tf), lambda i, f: (0, up_f(f))),
                  pl.BlockSpec((1, tf), lambda i, f: (0, nf + up_f(f))),
                  pl.BlockSpec((tf, D_MODEL), lambda i, f: (jnp.maximum(f - 1, 0), 0)),
                  pl.BlockSpec((tm, D_MODEL), lambda i, f: (i, 0)),
                  pl.BlockSpec((None, 1, D_MODEL), lambda i, f: (i // tps, 0, 5))],
        out_specs=pl.BlockSpec((tm, D_MODEL), lambda i, f: (i, 0)),
        out_shape=jax.ShapeDtypeStruct((m, D_MODEL), F32),
        scratch_shapes=[pltpu.VMEM((tm, D_MODEL), F32), pltpu.VMEM((2, tm, tf), BF16)],
        compiler_params=_cparams(("arbitrary", "arbitrary")), name="conv_mlp",
    )(h2, h2, u_init, u_init, w_up_p, w_up_p, cw_p, cw_p, cb_p, cb_p, w_down_p, x1, mod3)


def _pad_ff(a, axis):
    lo, hi = jnp.split(a, 2, axis=axis)
    pad = [(0, 0)] * a.ndim
    pad[axis] = (0, D_FF_PAD - D_FF)
    return jnp.concatenate([jnp.pad(lo, pad), jnp.pad(hi, pad)], axis=axis)


def _prep_weights(p):
    w_in = p['w_in']
    o_z, o_xbc, o_dt = 0, SSD_INNER, SSD_INNER + SSD_CONV_DIM
    o_q = o_dt + SSD_HEADS
    o_k, o_v = o_q + ATT_WIDTH, o_q + 2 * ATT_WIDTH
    o_f = o_q + 3 * ATT_WIDTH
    o_g = o_f + ATT_HEADS
    pad_small = LANES - SSD_HEADS - ATT_HEADS
    w = {
        'zx': w_in[:, o_z:o_dt].astype(BF16),
        'small': jnp.concatenate([w_in[:, o_dt:o_q], w_in[:, o_f:o_g],
                                  jnp.zeros((D_MODEL, pad_small), F32)], axis=1).astype(BF16),
        'b_small': jnp.concatenate([p['m_dt_bias'], p['f_bias'], jnp.zeros((pad_small,), F32)]).reshape(1, LANES),
        'q': w_in[:, o_q:o_k].astype(BF16),
        'k': w_in[:, o_k:o_v].astype(BF16),
        'v': w_in[:, o_v:o_f].astype(BF16),
        'gates': w_in[:, o_g:o_g + 2 * D_MODEL].astype(BF16),
        'pm': p['w_proj_m'].astype(BF16),
        'pf': p['w_proj_f'].astype(BF16),
        'out': p['w_out'].astype(BF16),
        'up': _pad_ff(p['w_up'], 1).astype(BF16),
        'ffn_cw': _pad_ff(p['ffn_conv_w'], 1),
        'ffn_cb': _pad_ff(p['ffn_conv_b'].reshape(1, 2 * D_FF), 1),
        'down': jnp.pad(p['w_down'], ((0, D_FF_PAD - D_FF), (0, 0))).astype(BF16),
        'a_log': jnp.pad(p['m_a_log'], (0, LANES - SSD_HEADS)).reshape(1, LANES),
        'd_row': jnp.repeat(p['m_d'], SSD_HEAD_DIM).reshape(1, SSD_INNER),
    }
    return w


def _layer(x, mod, k_past, v_past, logf_past, ssm_h0, mconv_buf, ffn_buf, p, w, cfg):
    bn, t_len, _ = x.shape
    m = bn * t_len
    past = 0 if k_past is None else k_past.shape[1]
    mod3 = mod.reshape(bn, 1, N_MOD * D_MODEL)
    x2 = x.reshape(m, D_MODEL)

    h = _prenorm_call(x2, mod3, p['norm1_w'], 0, 1, cfg['tm_norm'], t_len)
    dtf = _dtf_call(h, w['small'], w['b_small'], cfg['tm_proj'])
    dtf3 = dtf.reshape(bn, t_len, LANES)
    logf = dtf3[:, :, LOGF_LANE0:LOGF_LANE0 + ATT_HEADS]

    if past:
        lf_past = jnp.pad(logf_past.astype(F32), ((0, 0), (0, 0), (LOGF_LANE0, LANES - LOGF_LANE0 - ATT_HEADS)))
        cum_all = _cumsum_call(jnp.concatenate([lf_past, dtf3], axis=1))
        cum_past, cum_new = cum_all[:, :past], cum_all[:, past:]
    else:
        cum_all = cum_new = _cumsum_call(dtf3)

    zx = _mm_call(h, w['zx'], cfg['tm_zx'], 1024, "zx_proj")
    q = _q_call(h, w['q'], p['q_norm_w'].reshape(1, ATT_HEAD_DIM), cfg['tm_proj'])
    k_new, kaug = _k_call(h, w['k'], p['k_norm_w'].reshape(1, ATT_HEAD_DIM), cum_new.reshape(m, LANES), cfg['tm_proj'])
    v_new, vt4 = _v_call(h, w['v'], cfg['tm_v'], bn, t_len)

    q3 = q.reshape(bn, t_len, ATT_WIDTH)
    kaug3 = kaug.reshape(bn, t_len, 2 * ATT_WIDTH)
    tq, tk = cfg['tq'], cfg['tk']
    tpad = tq - t_len if past else 0
    if past:
        kaug_past = _kaug_call(k_past.reshape(bn * past, ATT_WIDTH).astype(F32), cum_past.reshape(bn * past, LANES),
                               cfg['tm_proj']).reshape(bn, past, 2 * ATT_WIDTH)
        kaug3 = jnp.concatenate([kaug_past, jnp.pad(kaug3, ((0, 0), (0, tpad), (0, 0)))], axis=1)
        vt_past = v_past.astype(BF16).transpose(0, 2, 3, 1)
        ones_rows = jnp.zeros((bn, ATT_HEADS, VT_ROWS - ATT_HEAD_DIM, past), BF16).at[:, :, 0, :].set(1.0)
        vt4 = jnp.concatenate([jnp.concatenate([vt_past, ones_rows], axis=2),
                               jnp.pad(vt4, ((0, 0), (0, 0), (0, 0), (0, tpad)))], axis=3)
        q3 = jnp.pad(q3, ((0, 0), (0, tpad), (0, 0)))
    js = _attn_block_starts(cum_all, p['q_norm_w'], p['k_norm_w'], tq, tk, past, (t_len + tpad) // tq)
    yf = _attn_call(js, q3, kaug3, vt4, tq, tk, cfg['sw'], past)[:, :t_len].reshape(m, ATT_WIDTH)

    zx3 = zx.reshape(bn, t_len, SSD_INNER + SSD_CONV_DIM)
    cs8 = jnp.pad(mconv_buf.astype(F32), ((0, 0), (SUBLANES_F32 - (SSD_CONV_W - 1), 0), (0, 0)))
    hpg = SSD_HEADS_PER_GROUP
    h0t = ssm_h0.astype(F32).reshape(bn, SSD_GROUPS, hpg, SSD_HEAD_DIM, SSD_STATE)
    h0t = h0t.transpose(0, 1, 4, 2, 3).reshape(bn, SSD_GROUPS, SSD_STATE, SSD_GROUP_W)
    ymn, h_t = _ssd_call(zx3, dtf3, cs8, h0t, p['m_conv_w'], p['m_conv_b'].reshape(1, SSD_CONV_DIM),
                         w['a_log'], w['d_row'], p['m_norm_w'].reshape(1, SSD_INNER), cfg['tb'], cfg['q'])
    new_h = h_t.reshape(bn, SSD_GROUPS, SSD_STATE, hpg, SSD_HEAD_DIM).transpose(0, 1, 3, 4, 2)
    new_h = new_h.reshape(bn, SSD_HEADS, SSD_HEAD_DIM, SSD_STATE)
    new_mconv = zx3[:, t_len - (SSD_CONV_W - 1):, SSD_INNER:]

    mix = _merge_call(h, ymn.reshape(m, SSD_INNER), yf, w['gates'], w['pm'], w['pf'], cfg['tm_proj'], 512)
    x1, h2 = _outproj_call(mix, w['out'], x2, mod3, p['norm2_w'], cfg['tm_norm'], t_len)

    u_init = jnp.pad(_pad_ff(ffn_buf.astype(F32), 2), ((0, 0), (FFN_HALO - (FFN_CONV_W - 1), 0), (0, 0)))
    y = _ffn_call(h2, u_init, w['up'], w['ffn_cw'], w['ffn_cb'], w['down'], x1, mod3, cfg['tm_ffn'], 512, t_len)

    tail_rows = h2.reshape(bn, t_len, D_MODEL)[:, t_len - (FFN_CONV_W - 1):].reshape(bn * (FFN_CONV_W - 1), D_MODEL)
    n_tail = tail_rows.shape[0]
    tail_pad = -n_tail % SUBLANES_BF16
    u_tail = _mm_call(jnp.pad(tail_rows, ((0, tail_pad), (0, 0))), w['up'], n_tail + tail_pad, 1408, "ffn_state")
    u_tail = u_tail[:n_tail]
    new_ffn = jnp.concatenate([u_tail[:, :D_FF], u_tail[:, D_FF_PAD:D_FF_PAD + D_FF]], axis=1)
    new_ffn = new_ffn.reshape(bn, FFN_CONV_W - 1, 2 * D_FF)

    return (y.reshape(bn, t_len, D_MODEL), k_new.reshape(bn, t_len, ATT_HEADS, ATT_HEAD_DIM),
            v_new.reshape(bn, t_len, ATT_HEADS, ATT_HEAD_DIM), logf, new_h, new_mconv, new_ffn)


PROMPT_CFG = dict(tm_norm=512, tm_proj=512, tm_v=512, tm_zx=1024, tq=1024, tk=512, sw=1024, tb=1024, q=128, tm_ffn=512)
SAMPLE_CFG = dict(tm_norm=64, tm_proj=512, tm_v=64, tm_zx=512, tq=128, tk=512, sw=128, tb=64, q=64, tm_ffn=64)


def kernel(x_prompt, x_sample, c_prompt, c_sample, cache_fox_k, cache_fox_v, cache_fox_logf, state_ssm, state_mamba_conv, state_ffn_conv, norm1_w, norm2_w, w_ada, b_ada, w_in, m_conv_w, m_conv_b, m_dt_bias, m_a_log, m_d, m_norm_w, f_bias, q_norm_w, k_norm_w, w_proj_m, w_proj_f, w_out, w_up, ffn_conv_w, ffn_conv_b, w_down):
    bp, bs = x_prompt.shape[0], x_sample.shape[0]
    depth = w_in.shape[0]
    xp, xs = x_prompt, x_sample
    c_all = jnp.concatenate([c_prompt, c_sample], axis=0)
    c_all = jnp.pad(c_all, ((0, -(bp + bs) % SUBLANES_BF16), (0, 0)))
    outs_p = [[] for _ in range(6)]
    outs_s = [[] for _ in range(6)]
    for l in range(depth):
        p = {
            'norm1_w': norm1_w[l], 'norm2_w': norm2_w[l], 'w_in': w_in[l], 'm_conv_w': m_conv_w[l],
            'm_conv_b': m_conv_b[l], 'm_dt_bias': m_dt_bias[l], 'm_a_log': m_a_log[l], 'm_d': m_d[l],
            'm_norm_w': m_norm_w[l], 'f_bias': f_bias[l], 'q_norm_w': q_norm_w[l], 'k_norm_w': k_norm_w[l],
            'w_proj_m': w_proj_m[l], 'w_proj_f': w_proj_f[l], 'w_out': w_out[l], 'w_up': w_up[l],
            'ffn_conv_w': ffn_conv_w[l], 'ffn_conv_b': ffn_conv_b[l], 'w_down': w_down[l],
        }
        w = _prep_weights(p)
        mod = _mod_call(c_all, w_ada[l], b_ada[l])
        dt = xp.dtype
        res_p = _layer(xp, mod[:bp], None, None, None,
                       jnp.zeros((bp, SSD_HEADS, SSD_HEAD_DIM, SSD_STATE), dt),
                       jnp.zeros((bp, SSD_CONV_W - 1, SSD_CONV_DIM), dt),
                       jnp.zeros((bp, FFN_CONV_W - 1, 2 * D_FF), dt), p, w, PROMPT_CFG)
        res_s = _layer(xs, mod[bp:bp + bs], cache_fox_k[l], cache_fox_v[l], cache_fox_logf[l], state_ssm[l],
                       state_mamba_conv[l], state_ffn_conv[l], p, w, SAMPLE_CFG)
        xp, xs = res_p[0], res_s[0]
        for i in range(6):
            outs_p[i].append(res_p[i + 1])
            outs_s[i].append(res_s[i + 1])
    k_p, v_p, lf_p, ssm_p, mc_p, fc_p = [jnp.stack(o, axis=0) for o in outs_p]
    k_s, v_s, lf_s, ssm_s, mc_s, fc_s = [jnp.stack(o, axis=0) for o in outs_s]
    return (xp, xs, k_p, v_p, lf_p, ssm_p, mc_p, fc_p, k_s, v_s, lf_s, ssm_s, mc_s, fc_s)
```

```python
import functools

import jax
import jax.numpy as jnp
from jax import lax
from jax.experimental import pallas as pl
from jax.experimental.pallas import tpu as pltpu

F32, BF16 = jnp.float32, jnp.bfloat16

D_MODEL = 2048
EPS = 1e-6
N_MOD = 6
SSD_INNER = 4096
SSD_HEADS = 64
SSD_HEAD_DIM = 64
SSD_GROUPS = 8
SSD_STATE = 128
SSD_CONV_W = 4
SSD_CONV_DIM = SSD_INNER + 2 * SSD_GROUPS * SSD_STATE
SSD_GROUP_W = SSD_INNER // SSD_GROUPS
SSD_HEADS_PER_GROUP = SSD_HEADS // SSD_GROUPS
ATT_HEADS = 16
ATT_HEAD_DIM = 128
ATT_WIDTH = ATT_HEADS * ATT_HEAD_DIM
ATT_SCALE = ATT_HEAD_DIM ** -0.5
D_FF = 5504
D_FF_PAD = 5632
FFN_CONV_W = 3
LANES = 128
SUBLANES_F32 = 8
SUBLANES_BF16 = 16
VMEM_LIMIT_BYTES = 56 * 1024 * 1024

DT_LANES = SSD_HEADS
LOGF_LANE0 = SSD_HEADS
NEG_BIG = -1e30
LOG2E = 1.4426950408889634
VT_ROWS = ATT_HEAD_DIM + SUBLANES_BF16
EXP_UNDERFLOW = 104.0


def _cparams(semantics):
    return pltpu.CompilerParams(dimension_semantics=semantics, vmem_limit_bytes=VMEM_LIMIT_BYTES)


def _silu(v):
    return v * jax.nn.sigmoid(v)


def _split3(v):
    hi = v.astype(BF16)
    r1 = v - hi.astype(F32)
    mid = r1.astype(BF16)
    lo = (r1 - mid.astype(F32)).astype(BF16)
    return hi, mid, lo


def _mod_kernel(c_ref, w_ref, b_ref, o_ref):
    c = c_ref[...]
    o_ref[...] = jnp.dot(_silu(c).astype(BF16), w_ref[...].astype(BF16),
                         preferred_element_type=F32) + b_ref[...]


def _mod_call(c_all, w_ada, b_ada):
    rows, n = c_all.shape[0], w_ada.shape[1]
    tn = 1024
    return pl.pallas_call(
        _mod_kernel,
        grid=(n // tn,),
        in_specs=[pl.BlockSpec((rows, D_MODEL), lambda j: (0, 0)),
                  pl.BlockSpec((D_MODEL, tn), lambda j: (0, j)),
                  pl.BlockSpec((1, tn), lambda j: (0, j))],
        out_specs=pl.BlockSpec((rows, tn), lambda j: (0, j)),
        out_shape=jax.ShapeDtypeStruct((rows, n), F32),
        compiler_params=_cparams(("arbitrary",)),
        name="ada_mod",
    )(c_all, w_ada, b_ada.reshape(1, n))


def _modulated_rms(x, nw, sc, sh):
    r = lax.rsqrt(jnp.mean(x * x, axis=-1, keepdims=True) + EPS)
    return (x * r) * nw * (1.0 + sc) + sh


def _prenorm_kernel(x_ref, nw_ref, sh_ref, sc_ref, o_ref):
    o_ref[...] = _modulated_rms(x_ref[...], nw_ref[...], sc_ref[...], sh_ref[...]).astype(BF16)


def _prenorm_call(x2, mod3, norm_w, seg_shift, seg_scale, tm, rows_per_batch):
    m = x2.shape[0]
    tpb = rows_per_batch // tm
    return pl.pallas_call(
        _prenorm_kernel,
        grid=(m // tm,),
        in_specs=[pl.BlockSpec((tm, D_MODEL), lambda i: (i, 0)),
                  pl.BlockSpec((1, D_MODEL), lambda i: (0, 0)),
                  pl.BlockSpec((None, 1, D_MODEL), lambda i: (i // tpb, 0, seg_shift)),
                  pl.BlockSpec((None, 1, D_MODEL), lambda i: (i // tpb, 0, seg_scale))],
        out_specs=pl.BlockSpec((tm, D_MODEL), lambda i: (i, 0)),
        out_shape=jax.ShapeDtypeStruct((m, D_MODEL), BF16),
        compiler_params=_cparams(("arbitrary",)),
        name="prenorm",
    )(x2, norm_w.reshape(1, D_MODEL), mod3, mod3)


def _mm_kernel(a_ref, w_ref, o_ref):
    o_ref[...] = jnp.dot(a_ref[...], w_ref[...], preferred_element_type=F32)


def _mm_call(a, w, tm, tn, name):
    m, k = a.shape
    n = w.shape[1]
    return pl.pallas_call(
        _mm_kernel,
        grid=(m // tm, n // tn),
        in_specs=[pl.BlockSpec((tm, k), lambda i, j: (i, 0)),
                  pl.BlockSpec((k, tn), lambda i, j: (0, j))],
        out_specs=pl.BlockSpec((tm, tn), lambda i, j: (i, j)),
        out_shape=jax.ShapeDtypeStruct((m, n), F32),
        compiler_params=_cparams(("arbitrary", "arbitrary")),
        name=name,
    )(a, w)


def _dtf_kernel(h_ref, w_ref, b_ref, o_ref):
    v = jnp.dot(h_ref[...], w_ref[...], preferred_element_type=F32) + b_ref[...]
    lane = lax.broadcasted_iota(jnp.int32, v.shape, 1)
    tail = jnp.log1p(jnp.exp(-jnp.abs(v)))
    softplus = jnp.maximum(v, 0.0) + tail
    log_sigmoid = jnp.minimum(v, 0.0) - tail
    o_ref[...] = jnp.where(lane < DT_LANES, softplus,
                           jnp.where(lane < LOGF_LANE0 + ATT_HEADS, log_sigmoid, 0.0))


def _dtf_call(h, w_small, b_small, tm):
    m = h.shape[0]
    return pl.pallas_call(
        _dtf_kernel,
        grid=(m // tm,),
        in_specs=[pl.BlockSpec((tm, D_MODEL), lambda i: (i, 0)),
                  pl.BlockSpec((D_MODEL, LANES), lambda i: (0, 0)),
                  pl.BlockSpec((1, LANES), lambda i: (0, 0))],
        out_specs=pl.BlockSpec((tm, LANES), lambda i: (i, 0)),
        out_shape=jax.ShapeDtypeStruct((m, LANES), F32),
        compiler_params=_cparams(("arbitrary",)),
        name="dt_logf_proj",
    )(h, w_small, b_small)


CUM_ROWS = 64


def _tri(n, lower):
    r = lax.broadcasted_iota(jnp.int32, (n, n), 0)
    c = lax.broadcasted_iota(jnp.int32, (n, n), 1)
    return (c <= r) if lower else (r <= c)


def _cumsum_rows(x, tril3):
    hi, mid, lo = _split3(x)
    return jnp.dot(tril3, jnp.concatenate([hi, mid, lo], axis=0), preferred_element_type=F32)


def _cumsum_kernel(x_ref, o_ref, *, n_chunks):
    tril = _tri(CUM_ROWS, True).astype(BF16)
    tril3 = jnp.concatenate([tril, tril, tril], axis=1)

    def body(c, carry):
        s = pl.multiple_of(c * CUM_ROWS, CUM_ROWS)
        y = _cumsum_rows(x_ref[pl.ds(s, CUM_ROWS), :], tril3) + carry
        o_ref[pl.ds(s, CUM_ROWS), :] = y
        return y[CUM_ROWS - 1:CUM_ROWS, :]

    lax.fori_loop(0, n_chunks, body, jnp.zeros((1, LANES), F32))


def _cumsum_call(x3):
    bn, length, _ = x3.shape
    return pl.pallas_call(
        functools.partial(_cumsum_kernel, n_chunks=length // CUM_ROWS),
        grid=(bn,),
        in_specs=[pl.BlockSpec((None, length, LANES), lambda b: (b, 0, 0))],
        out_specs=pl.BlockSpec((None, length, LANES), lambda b: (b, 0, 0)),
        out_shape=jax.ShapeDtypeStruct(x3.shape, F32),
        compiler_params=_cparams(("arbitrary",)),
        name="logf_cumsum",
    )(x3)


def _head_rms(blk, w_row):
    ms = jnp.mean(blk * blk, axis=-1, keepdims=True)
    return blk * lax.rsqrt(ms + EPS) * w_row


def _key_bias_columns(cum, head, rows):
    lane = lax.broadcasted_iota(jnp.int32, (rows, LANES), 1)
    c = jnp.broadcast_to(cum[:, LOGF_LANE0 + head:LOGF_LANE0 + head + 1] * (-LOG2E), (rows, LANES))
    hi, mid, lo = _split3(c)
    return jnp.where(lane == 0, hi.astype(F32),
                     jnp.where(lane == 1, mid.astype(F32),
                               jnp.where(lane == 2, lo.astype(F32), 0.0)))


def _q_kernel(h_ref, w_ref, nw_ref, q_ref):
    acc = jnp.dot(h_ref[...], w_ref[...], preferred_element_type=F32)
    for hh in range(ATT_HEADS):
        sl = slice(hh * ATT_HEAD_DIM, (hh + 1) * ATT_HEAD_DIM)
        q_ref[:, sl] = (_head_rms(acc[:, sl], nw_ref[...]) * (ATT_SCALE * LOG2E)).astype(BF16)


def _k_kernel(h_ref, w_ref, nw_ref, cum_ref, k_ref, kaug_ref):
    acc = jnp.dot(h_ref[...], w_ref[...], preferred_element_type=F32)
    rows = acc.shape[0]
    cum = cum_ref[...]
    for hh in range(ATT_HEADS):
        sl = slice(hh * ATT_HEAD_DIM, (hh + 1) * ATT_HEAD_DIM)
        kn = _head_rms(acc[:, sl], nw_ref[...])
        k_ref[:, sl] = kn
        kaug_ref[:, 2 * hh * LANES:(2 * hh + 1) * LANES] = kn.astype(BF16)
        kaug_ref[:, (2 * hh + 1) * LANES:(2 * hh + 2) * LANES] = _key_bias_columns(cum, hh, rows).astype(BF16)


def _kaug_kernel(k_ref, cum_ref, kaug_ref):
    rows = k_ref.shape[0]
    cum = cum_ref[...]
    for hh in range(ATT_HEADS):
        sl = slice(hh * ATT_HEAD_DIM, (hh + 1) * ATT_HEAD_DIM)
        kaug_ref[:, 2 * hh * LANES:(2 * hh + 1) * LANES] = k_ref[:, sl].astype(BF16)
        kaug_ref[:, (2 * hh + 1) * LANES:(2 * hh + 2) * LANES] = _key_bias_columns(cum, hh, rows).astype(BF16)


def _v_kernel(h_ref, w_ref, v_ref, vt_ref):
    acc = jnp.dot(h_ref[...], w_ref[...], preferred_element_type=F32)
    tm = acc.shape[0]
    v_ref[...] = acc
    row = lax.broadcasted_iota(jnp.int32, (VT_ROWS - ATT_HEAD_DIM, tm), 0)
    ones_rows = jnp.where(row == 0, 1.0, 0.0).astype(BF16)
    for hh in range(ATT_HEADS):
        vt_ref[hh, 0:ATT_HEAD_DIM, :] = acc[:, hh * ATT_HEAD_DIM:(hh + 1) * ATT_HEAD_DIM].T.astype(BF16)
        vt_ref[hh, ATT_HEAD_DIM:VT_ROWS, :] = ones_rows


def _row_spec(tm, n):
    return pl.BlockSpec((tm, n), lambda i: (i, 0))


def _const_spec(r, n):
    return pl.BlockSpec((r, n), lambda i: (0, 0))


def _q_call(h, w, nw, tm):
    m = h.shape[0]
    return pl.pallas_call(
        _q_kernel, grid=(m // tm,),
        in_specs=[_row_spec(tm, D_MODEL), _const_spec(D_MODEL, ATT_WIDTH), _const_spec(1, ATT_HEAD_DIM)],
        out_specs=_row_spec(tm, ATT_WIDTH),
        out_shape=jax.ShapeDtypeStruct((m, ATT_WIDTH), BF16),
        compiler_params=_cparams(("arbitrary",)), name="q_proj",
    )(h, w, nw)


def _k_call(h, w, nw, cum2, tm):
    m = h.shape[0]
    return pl.pallas_call(
        _k_kernel, grid=(m // tm,),
        in_specs=[_row_spec(tm, D_MODEL), _const_spec(D_MODEL, ATT_WIDTH), _const_spec(1, ATT_HEAD_DIM),
                  _row_spec(tm, LANES)],
        out_specs=[_row_spec(tm, ATT_WIDTH), _row_spec(tm, 2 * ATT_WIDTH)],
        out_shape=[jax.ShapeDtypeStruct((m, ATT_WIDTH), F32), jax.ShapeDtypeStruct((m, 2 * ATT_WIDTH), BF16)],
        compiler_params=_cparams(("arbitrary",)), name="k_proj",
    )(h, w, nw, cum2)


def _kaug_call(k2, cum2, tm):
    m = k2.shape[0]
    return pl.pallas_call(
        _kaug_kernel, grid=(m // tm,),
        in_specs=[_row_spec(tm, ATT_WIDTH), _row_spec(tm, LANES)],
        out_specs=_row_spec(tm, 2 * ATT_WIDTH),
        out_shape=jax.ShapeDtypeStruct((m, 2 * ATT_WIDTH), BF16),
        compiler_params=_cparams(("arbitrary",)), name="k_cache_aug",
    )(k2, cum2)


def _v_call(h, w, tm, bn, t_len):
    m = h.shape[0]
    tpb = t_len // tm
    return pl.pallas_call(
        _v_kernel, grid=(m // tm,),
        in_specs=[_row_spec(tm, D_MODEL), _const_spec(D_MODEL, ATT_WIDTH)],
        out_specs=[_row_spec(tm, ATT_WIDTH),
                   pl.BlockSpec((None, ATT_HEADS, VT_ROWS, tm), lambda i: (i // tpb, 0, 0, i % tpb))],
        out_shape=[jax.ShapeDtypeStruct((m, ATT_WIDTH), F32),
                   jax.ShapeDtypeStruct((bn, ATT_HEADS, VT_ROWS, t_len), BF16)],
        compiler_params=_cparams(("arbitrary",)), name="v_proj",
    )(h, w)


def _attn_kernel(js_ref, q_ref, k_ref, vt_ref, o_ref, acc_ref, s_ref, *, tq, tk, n_diag, past):
    b, h, i = pl.program_id(0), pl.program_id(1), pl.program_id(2)
    nq = pl.num_programs(2)
    lane = lax.broadcasted_iota(jnp.int32, (tq, LANES), 1)
    ones = jnp.where(lane < 3, 1.0, 0.0).astype(BF16)
    qa = jnp.concatenate([q_ref[...], ones], axis=1)
    acc_ref[...] = jnp.zeros(acc_ref.shape, F32)

    def block(kb, vt_blk, m, c0, masked):
        n = kb.shape[0]
        st = lax.dot_general(kb, qa[c0:, :], (((1,), (1,)), ((), ())), preferred_element_type=F32)
        if masked:
            r = lax.broadcasted_iota(jnp.int32, st.shape, 0)
            c = lax.broadcasted_iota(jnp.int32, st.shape, 1)
            st = jnp.where(r <= c, st, NEG_BIG)
        s_ref[0:n, c0:] = st
        m_new = jnp.maximum(m, jnp.max(st, axis=0, keepdims=True))
        alpha = jnp.exp2(m - m_new)
        p = jnp.exp2(s_ref[0:n, c0:] - m_new).astype(BF16)
        pv = jnp.dot(vt_blk, p, preferred_element_type=F32)
        acc_ref[:, c0:] = alpha * acc_ref[:, c0:] + pv
        return m_new

    def body(j, m):
        s = pl.multiple_of(j * tk, tk)
        return block(k_ref[pl.ds(s, tk), :], vt_ref[:, pl.ds(s, tk)], m, 0, False)

    j_start = js_ref[(b * ATT_HEADS + h) * nq + i]
    m = lax.fori_loop(j_start, past // tk + i * (tq // tk), body, jnp.full((1, tq), NEG_BIG, F32))

    hw = tq // n_diag
    for d in range(n_diag):
        d0 = pl.multiple_of(past + i * tq + d * hw, hw)
        m_hi = block(k_ref[pl.ds(d0, hw), :], vt_ref[:, pl.ds(d0, hw)], m[:, d * hw:], d * hw, True)
        m = m_hi if d == 0 else jnp.concatenate([m[:, :d * hw], m_hi], axis=1)

    acc = acc_ref[...]
    o = acc[0:ATT_HEAD_DIM, :] * (1.0 / acc[ATT_HEAD_DIM:ATT_HEAD_DIM + 1, :])
    o_ref[...] = o.T.astype(BF16)


def _attn_block_starts(cum_all, q_norm_w, k_norm_w, tq, tk, past, nq):
    bn = cum_all.shape[0]
    bound = ATT_HEAD_DIM * jnp.max(jnp.abs(q_norm_w)) * jnp.max(jnp.abs(k_norm_w)) * ATT_SCALE
    thr = EXP_UNDERFLOW + 2.0 * bound
    c = cum_all[:, :, LOGF_LANE0:LOGF_LANE0 + ATT_HEADS]
    n_off = past // tk + jnp.arange(nq) * (tq // tk)
    nb = past // tk + (nq - 1) * (tq // tk)
    if nb == 0:
        return jnp.zeros((bn * ATT_HEADS * nq,), jnp.int32)
    t_real = c.shape[1] - past
    cq = jnp.pad(c[:, past:], ((0, 0), (0, nq * tq - t_real), (0, 0)), constant_values=-jnp.inf)
    cq_max = jnp.max(cq.reshape(bn, nq, tq, ATT_HEADS), axis=2)
    ck_min = jnp.min(c[:, :nb * tk].reshape(bn, nb, tk, ATT_HEADS), axis=2)
    gap = cq_max[:, :, None, :] - ck_min[:, None, :, :]
    jj = jnp.arange(nb)[None, None, :, None]
    cand = jnp.where((gap >= -thr) & (jj < n_off[None, :, None, None]), jj, n_off[None, :, None, None])
    js = jnp.min(cand, axis=2)
    return js.transpose(0, 2, 1).reshape(-1).astype(jnp.int32)


def _attn_call(js, q3, kaug3, vt4, tq, tk, n_diag, past):
    bn, t_q, _ = q3.shape
    t_k = kaug3.shape[1]
    grid_spec = pltpu.PrefetchScalarGridSpec(
        num_scalar_prefetch=1,
        grid=(bn, ATT_HEADS, t_q // tq),
        in_specs=[pl.BlockSpec((None, tq, ATT_HEAD_DIM), lambda b, h, i, js_ref: (b, i, h)),
                  pl.BlockSpec((None, t_k, 2 * ATT_HEAD_DIM), lambda b, h, i, js_ref: (b, 0, h)),
                  pl.BlockSpec((None, None, VT_ROWS, t_k), lambda b, h, i, js_ref: (b, h, 0, 0))],
        out_specs=pl.BlockSpec((None, tq, ATT_HEAD_DIM), lambda b, h, i, js_ref: (b, i, h)),
        scratch_shapes=[pltpu.VMEM((VT_ROWS, tq), F32), pltpu.VMEM((max(tk, tq // n_diag), tq), F32)])
    return pl.pallas_call(
        functools.partial(_attn_kernel, tq=tq, tk=tk, n_diag=n_diag, past=past),
        grid_spec=grid_spec,
        out_shape=jax.ShapeDtypeStruct((bn, t_q, ATT_WIDTH), BF16),
        compiler_params=_cparams(("arbitrary", "arbitrary", "arbitrary")),
        name="fox_attention",
    )(js, q3, kaug3, vt4)


def _ssd_kernel(x_ref, b_ref, c_ref, z_ref, dt_ref, csx_ref, csb_ref, csc_ref, h0_ref,
                cwx_ref, cwb_ref, cwc_ref, cbx_ref, cbb_ref, cbc_ref, alog_ref, d_ref, nw_ref,
                y_ref, hout_ref,
                ext_x, ext_b, ext_c, xs, bs, cs, h_s, *, tb, q, n_chunks, n_tblocks):
    g = pl.program_id(1)
    t = pl.program_id(2)
    halo = SUBLANES_F32

    n_slabs = SSD_GROUP_W // LANES

    @pl.when(t == 0)
    def _():
        ext_x[0:halo, :] = csx_ref[...]
        ext_b[0:halo, :] = csb_ref[...]
        ext_c[0:halo, :] = csc_ref[...]
        h_s[...] = h0_ref[...]

    def conv(ext, raw_ref, cw_ref, cb_ref):
        ext[halo:halo + tb, :] = raw_ref[...]
        e = ext[...]
        y = cb_ref[...]
        for j in range(SSD_CONV_W):
            back = SSD_CONV_W - 1 - j
            win = e if back == 0 else pltpu.roll(e, back, 0)
            y = y + win[halo:halo + tb, :] * cw_ref[j:j + 1, :]
        ext[0:halo, :] = e[tb:tb + halo, :]
        return _silu(y)

    xs[...] = conv(ext_x, x_ref, cwx_ref, cbx_ref)
    bs[...] = conv(ext_b, b_ref, cwb_ref, cbb_ref).astype(BF16)
    cs[...] = conv(ext_c, c_ref, cwc_ref, cbc_ref).astype(BF16)

    lane_row = lax.broadcasted_iota(jnp.int32, (1, LANES), 1)
    a_row = jnp.where(lane_row < DT_LANES, -jnp.exp(alog_ref[...]), 0.0)
    shift = lax.rem(LANES - SSD_HEADS_PER_GROUP * g, LANES)
    tril_mask = _tri(q, True)
    tril = tril_mask.astype(BF16)
    tril3 = jnp.concatenate([tril, tril, tril], axis=1)
    lo_half = lax.broadcasted_iota(jnp.int32, (q, LANES), 1) < SSD_HEAD_DIM

    def chunk(ci, carry):
        s = pl.multiple_of(ci * q, q)
        dt_all = dt_ref[pl.ds(s, q), :]
        cum_all = _cumsum_rows(dt_all * a_row, tril3)
        cum_g = pltpu.roll(cum_all, shift, 1)
        dt_g = pltpu.roll(dt_all, shift, 1)
        cum_t = cum_g.T

        bc = bs[pl.ds(s, q), :]
        cc = cs[pl.ds(s, q), :]
        cb = lax.dot_general(cc, bc, (((1,), (1,)), ((), ())), preferred_element_type=F32)
        h_t = h_s[...]
        y_inter = jnp.dot(cc, h_t.astype(BF16), preferred_element_type=F32)

        y_slabs, xd_slabs, etot_slabs = [], [], []
        for k in range(n_slabs):
            h0i, h1i = 2 * k, 2 * k + 1
            xck = xs[pl.ds(s, q), k * LANES:(k + 1) * LANES]
            cum_b = {hh: jnp.broadcast_to(cum_g[:, hh:hh + 1], (q, LANES)) for hh in (h0i, h1i)}
            dt_b = {hh: jnp.broadcast_to(dt_g[:, hh:hh + 1], (q, LANES)) for hh in (h0i, h1i)}
            cum_pair = jnp.where(lo_half, cum_b[h0i], cum_b[h1i])
            last_pair = cum_pair[q - 1:q, :]
            xdt = xck * jnp.where(lo_half, dt_b[h0i], dt_b[h1i])
            y_slab = y_inter[:, k * LANES:(k + 1) * LANES] * jnp.exp(cum_pair) + xck * d_ref[:, k * LANES:(k + 1) * LANES]
            for hi_, sel in ((h0i, lo_half), (h1i, jnp.logical_not(lo_half))):
                seg = cum_b[hi_][:, 0:q] - cum_t[hi_:hi_ + 1, :]
                lmat = jnp.exp(jnp.where(tril_mask, seg, -jnp.inf))
                mh = (cb * lmat).astype(BF16)
                rhs = jnp.where(sel, xdt, 0.0).astype(BF16)
                y_slab = y_slab + jnp.dot(mh, rhs, preferred_element_type=F32)
            y_slabs.append(y_slab)
            xd_slabs.append((xdt * jnp.exp(last_pair - cum_pair)).astype(BF16))
            etot_slabs.append(jnp.exp(last_pair))

        xd = jnp.concatenate(xd_slabs, axis=1)
        h_s[...] = h_t * jnp.concatenate(etot_slabs, axis=1) + lax.dot_general(
            bc, xd, (((0,), (0,)), ((), ())), preferred_element_type=F32)

        y = jnp.concatenate(y_slabs, axis=1) * _silu(z_ref[pl.ds(s, q), :])
        y = y * lax.rsqrt(jnp.mean(y * y, axis=-1, keepdims=True) + EPS) * nw_ref[...]
        y_ref[pl.ds(s, q), :] = y.astype(BF16)
        return carry

    lax.fori_loop(0, n_chunks, chunk, 0, unroll=min(2, n_chunks))

    @pl.when(t == n_tblocks - 1)
    def _():
        hout_ref[...] = h_s[...]


def _ssd_call(zx3, dtf3, cs8, h0t, conv_w, conv_b, a_log_row, d_row, norm_w_row, tb, q):
    bn, t_len, _ = zx3.shape
    n_tblocks = t_len // tb
    gw, st = SSD_GROUP_W, SSD_STATE
    zx_x0, zx_b0, zx_c0 = SSD_INNER // gw, 2 * SSD_INNER // st, 2 * SSD_INNER // st + SSD_GROUPS
    cv_b0, cv_c0 = SSD_INNER // st, SSD_INNER // st + SSD_GROUPS

    def rows(width, col0):
        return pl.BlockSpec((None, tb, width), lambda b, g, t: (b, t, col0 + g))

    def per_group(r, width, col0):
        return pl.BlockSpec((r, width), lambda b, g, t: (0, col0 + g))

    def tail(width, col0):
        return pl.BlockSpec((None, SUBLANES_F32, width), lambda b, g, t: (b, 0, col0 + g))

    in_specs = [
        rows(gw, zx_x0), rows(st, zx_b0), rows(st, zx_c0), rows(gw, 0),
        pl.BlockSpec((None, tb, LANES), lambda b, g, t: (b, t, 0)),
        tail(gw, 0), tail(st, cv_b0), tail(st, cv_c0),
        pl.BlockSpec((None, None, st, gw), lambda b, g, t: (b, g, 0, 0)),
        per_group(SSD_CONV_W, gw, 0), per_group(SSD_CONV_W, st, cv_b0), per_group(SSD_CONV_W, st, cv_c0),
        per_group(1, gw, 0), per_group(1, st, cv_b0), per_group(1, st, cv_c0),
        pl.BlockSpec((1, LANES), lambda b, g, t: (0, 0)),
        per_group(1, gw, 0), per_group(1, gw, 0),
    ]
    out_specs = [pl.BlockSpec((None, tb, gw), lambda b, g, t: (b, t, g)),
                 pl.BlockSpec((None, None, st, gw), lambda b, g, t: (b, g, 0, 0))]
    out_shape = [jax.ShapeDtypeStruct((bn, t_len, SSD_INNER), BF16),
                 jax.ShapeDtypeStruct((bn, SSD_GROUPS, st, gw), F32)]
    scratch = [pltpu.VMEM((tb + SUBLANES_F32, gw), F32), pltpu.VMEM((tb + SUBLANES_F32, st), F32),
               pltpu.VMEM((tb + SUBLANES_F32, st), F32),
               pltpu.VMEM((tb, gw), F32), pltpu.VMEM((tb, st), BF16), pltpu.VMEM((tb, st), BF16),
               pltpu.VMEM((st, gw), F32)]
    return pl.pallas_call(
        functools.partial(_ssd_kernel, tb=tb, q=q, n_chunks=tb // q, n_tblocks=n_tblocks),
        grid=(bn, SSD_GROUPS, n_tblocks),
        in_specs=in_specs, out_specs=out_specs, out_shape=out_shape, scratch_shapes=scratch,
        compiler_params=_cparams(("arbitrary", "arbitrary", "arbitrary")),
        name="ssd_scan",
    )(zx3, zx3, zx3, zx3, dtf3, cs8, cs8, cs8, h0t, conv_w, conv_w, conv_w, conv_b, conv_b, conv_b,
      a_log_row, d_row, norm_w_row)


def _merge_kernel(h_ref, ym_ref, yf_ref, wgm_ref, wgf_ref, wm_ref, wf_ref, o_ref):
    h = h_ref[...]
    gm = jax.nn.sigmoid(jnp.dot(h, wgm_ref[...], preferred_element_type=F32))
    gf = jax.nn.sigmoid(jnp.dot(h, wgf_ref[...], preferred_element_type=F32))
    a = jnp.dot(ym_ref[...], wm_ref[...], preferred_element_type=F32)
    b = jnp.dot(yf_ref[...], wf_ref[...], preferred_element_type=F32)
    o_ref[...] = (gm * a + gf * b).astype(BF16)


def _merge_call(h, ymn, yf, w_gates, w_pm, w_pf, tm, tn):
    m = h.shape[0]
    n_tiles = D_MODEL // tn
    return pl.pallas_call(
        _merge_kernel, grid=(m // tm, n_tiles),
        in_specs=[pl.BlockSpec((tm, D_MODEL), lambda i, j: (i, 0)),
                  pl.BlockSpec((tm, SSD_INNER), lambda i, j: (i, 0)),
                  pl.BlockSpec((tm, ATT_WIDTH), lambda i, j: (i, 0)),
                  pl.BlockSpec((D_MODEL, tn), lambda i, j: (0, j)),
                  pl.BlockSpec((D_MODEL, tn), lambda i, j: (0, n_tiles + j)),
                  pl.BlockSpec((SSD_INNER, tn), lambda i, j: (0, j)),
                  pl.BlockSpec((ATT_WIDTH, tn), lambda i, j: (0, j))],
        out_specs=pl.BlockSpec((tm, tn), lambda i, j: (i, j)),
        out_shape=jax.ShapeDtypeStruct((m, D_MODEL), BF16),
        compiler_params=_cparams(("arbitrary", "arbitrary")), name="gated_merge",
    )(h, ymn, yf, w_gates, w_gates, w_pm, w_pf)


def _outproj_kernel(mix_ref, w_ref, x_ref, g_ref, nw_ref, sh_ref, sc_ref, x1_ref, h2_ref):
    y = jnp.dot(mix_ref[...], w_ref[...], preferred_element_type=F32)
    x1 = x_ref[...] + g_ref[...] * y
    x1_ref[...] = x1
    h2_ref[...] = _modulated_rms(x1, nw_ref[...], sc_ref[...], sh_ref[...]).astype(BF16)


def _outproj_call(mix, w_out, x2, mod3, norm2_w, tm, rows_per_batch):
    m = x2.shape[0]
    tpb = rows_per_batch // tm

    def mod_seg(seg):
        return pl.BlockSpec((None, 1, D_MODEL), lambda i: (i // tpb, 0, seg))

    return pl.pallas_call(
        _outproj_kernel, grid=(m // tm,),
        in_specs=[_row_spec(tm, D_MODEL), _const_spec(D_MODEL, D_MODEL), _row_spec(tm, D_MODEL),
                  mod_seg(2), _const_spec(1, D_MODEL), mod_seg(3), mod_seg(4)],
        out_specs=[_row_spec(tm, D_MODEL), _row_spec(tm, D_MODEL)],
        out_shape=[jax.ShapeDtypeStruct((m, D_MODEL), F32), jax.ShapeDtypeStruct((m, D_MODEL), BF16)],
        compiler_params=_cparams(("arbitrary",)), name="out_proj",
    )(mix, w_out, x2, mod3, norm2_w.reshape(1, D_MODEL), mod3, mod3)


FFN_HALO = SUBLANES_BF16
FFN_SUB_ROWS = 256


def _ffn_kernel(h_ref, halo_ref, uia_ref, uib_ref, wa_ref, wb_ref, cwa_ref, cwb_ref, cba_ref, cbb_ref,
                wd_ref, x_ref, g_ref, o_ref, acc_ref, exta_ref, extb_ref, *, tm, tiles_per_seq, nf):
    i = pl.program_id(0)
    f = pl.program_id(1)
    first = (i % tiles_per_seq) == 0

    @pl.when(f == 0)
    def _():
        acc_ref[...] = jnp.zeros(acc_ref.shape, F32)

    h = h_ref[...]
    hprev = halo_ref[...]

    for ext_ref, w_ref, ui_ref in ((exta_ref, wa_ref, uia_ref), (extb_ref, wb_ref, uib_ref)):
        uh = jnp.dot(hprev, w_ref[...], preferred_element_type=F32)
        ext_ref[0:FFN_HALO, :] = jnp.where(first, ui_ref[...], uh)
        ext_ref[FFN_HALO:FFN_HALO + tm, :] = jnp.dot(h, w_ref[...], preferred_element_type=F32)

    def conv(ext_ref, cw_ref, cb_ref, r0, r1):
        y = cb_ref[...]
        for j in range(FFN_CONV_W):
            off = FFN_HALO - (FFN_CONV_W - 1) + j
            y = y + ext_ref[off + r0:off + r1, :] * cw_ref[j:j + 1, :]
        return y

    ua = _silu(conv(exta_ref, cwa_ref, cba_ref, 0, tm))
    rt = min(tm, FFN_SUB_ROWS)
    for r0 in range(0, tm, rt):
        ub = conv(extb_ref, cwb_ref, cbb_ref, r0, r0 + rt)
        gact = (ua[r0:r0 + rt, :] * ub).astype(BF16)
        acc_ref[r0:r0 + rt, :] += jnp.dot(gact, wd_ref[...], preferred_element_type=F32)

    @pl.when(f == nf - 1)
    def _():
        o_ref[...] = x_ref[...] + g_ref[...] * acc_ref[...]


def _ffn_call(h2, u_init, w_up_p, cw_p, cb_p, w_down_p, x1, mod3, tm, tf, rows_per_seq):
    m = h2.shape[0]
    nf = D_FF_PAD // tf
    tps = rows_per_seq // tm
    halo_blocks = tm // FFN_HALO
    return pl.pallas_call(
        functools.partial(_ffn_kernel, tm=tm, tiles_per_seq=tps, nf=nf),
        grid=(m // tm, nf),
        in_specs=[pl.BlockSpec((tm, D_MODEL), lambda i, f: (i, 0)),
                  pl.BlockSpec((FFN_HALO, D_MODEL), lambda i, f: (jnp.maximum(i * halo_blocks - 1, 0), 0)),
                  pl.BlockSpec((None, FFN_HALO, tf), lambda i, f: (i // tps, 0, f)),
                  pl.BlockSpec((None, FFN_HALO, tf), lambda i, f: (i // tps, 0, nf + f)),
                  pl.BlockSpec((D_MODEL, tf), lambda i, f: (0, f)),
                  pl.BlockSpec((D_MODEL, tf), lambda i, f: (0, nf + f)),
                  pl.BlockSpec((FFN_CONV_W, tf), lambda i, f: (0, f)),
                  pl.BlockSpec((FFN_CONV_W, tf), lambda i, f: (0, nf + f)),
                  pl.BlockSpec((1, tf), lambda i, f: (0, f)),
                  pl.BlockSpec((1, tf), lambda i, f: (0, nf + f)),
                  pl.BlockSpec((tf, D_MODEL), lambda i, f: (f, 0)),
                  pl.BlockSpec((tm, D_MODEL), lambda i, f: (i, 0)),
                  pl.BlockSpec((None, 1, D_MODEL), lambda i, f: (i // tps, 0, 5))],
        out_specs=pl.BlockSpec((tm, D_MODEL), lambda i, f: (i, 0)),
        out_shape=jax.ShapeDtypeStruct((m, D_MODEL), F32),
        scratch_shapes=[pltpu.VMEM((tm, D_MODEL), F32), pltpu.VMEM((tm + FFN_HALO, tf), F32),
                        pltpu.VMEM((tm + FFN_HALO, tf), F32)],
        compiler_params=_cparams(("arbitrary", "arbitrary")), name="conv_mlp",
    )(h2, h2, u_init, u_init, w_up_p, w_up_p, cw_p, cw_p, cb_p, cb_p, w_down_p, x1, mod3)


def _pad_ff(a, axis):
    lo, hi = jnp.split(a, 2, axis=axis)
    pad = [(0, 0)] * a.ndim
    pad[axis] = (0, D_FF_PAD - D_FF)
    return jnp.concatenate([jnp.pad(lo, pad), jnp.pad(hi, pad)], axis=axis)


def _prep_weights(p):
    w_in = p['w_in']
    o_z, o_xbc, o_dt = 0, SSD_INNER, SSD_INNER + SSD_CONV_DIM
    o_q = o_dt + SSD_HEADS
    o_k, o_v = o_q + ATT_WIDTH, o_q + 2 * ATT_WIDTH
    o_f = o_q + 3 * ATT_WIDTH
    o_g = o_f + ATT_HEADS
    pad_small = LANES - SSD_HEADS - ATT_HEADS
    w = {
        'zx': w_in[:, o_z:o_dt].astype(BF16),
        'small': jnp.concatenate([w_in[:, o_dt:o_q], w_in[:, o_f:o_g],
                                  jnp.zeros((D_MODEL, pad_small), F32)], axis=1).astype(BF16),
        'b_small': jnp.concatenate([p['m_dt_bias'], p['f_bias'], jnp.zeros((pad_small,), F32)]).reshape(1, LANES),
        'q': w_in[:, o_q:o_k].astype(BF16),
        'k': w_in[:, o_k:o_v].astype(BF16),
        'v': w_in[:, o_v:o_f].astype(BF16),
        'gates': w_in[:, o_g:o_g + 2 * D_MODEL].astype(BF16),
        'pm': p['w_proj_m'].astype(BF16),
        'pf': p['w_proj_f'].astype(BF16),
        'out': p['w_out'].astype(BF16),
        'up': _pad_ff(p['w_up'], 1).astype(BF16),
        'ffn_cw': _pad_ff(p['ffn_conv_w'], 1),
        'ffn_cb': _pad_ff(p['ffn_conv_b'].reshape(1, 2 * D_FF), 1),
        'down': jnp.pad(p['w_down'], ((0, D_FF_PAD - D_FF), (0, 0))).astype(BF16),
        'a_log': jnp.pad(p['m_a_log'], (0, LANES - SSD_HEADS)).reshape(1, LANES),
        'd_row': jnp.repeat(p['m_d'], SSD_HEAD_DIM).reshape(1, SSD_INNER),
    }
    return w


def _layer(x, mod, k_past, v_past, logf_past, ssm_h0, mconv_buf, ffn_buf, p, w, cfg):
    bn, t_len, _ = x.shape
    m = bn * t_len
    past = 0 if k_past is None else k_past.shape[1]
    mod3 = mod.reshape(bn, 1, N_MOD * D_MODEL)
    x2 = x.reshape(m, D_MODEL)

    h = _prenorm_call(x2, mod3, p['norm1_w'], 0, 1, cfg['tm_norm'], t_len)
    dtf = _dtf_call(h, w['small'], w['b_small'], cfg['tm_proj'])
    dtf3 = dtf.reshape(bn, t_len, LANES)
    logf = dtf3[:, :, LOGF_LANE0:LOGF_LANE0 + ATT_HEADS]

    if past:
        lf_past = jnp.pad(logf_past.astype(F32), ((0, 0), (0, 0), (LOGF_LANE0, LANES - LOGF_LANE0 - ATT_HEADS)))
        cum_all = _cumsum_call(jnp.concatenate([lf_past, dtf3], axis=1))
        cum_past, cum_new = cum_all[:, :past], cum_all[:, past:]
    else:
        cum_all = cum_new = _cumsum_call(dtf3)

    zx = _mm_call(h, w['zx'], cfg['tm_zx'], 1024, "zx_proj")
    q = _q_call(h, w['q'], p['q_norm_w'].reshape(1, ATT_HEAD_DIM), cfg['tm_proj'])
    k_new, kaug = _k_call(h, w['k'], p['k_norm_w'].reshape(1, ATT_HEAD_DIM), cum_new.reshape(m, LANES), cfg['tm_proj'])
    v_new, vt4 = _v_call(h, w['v'], cfg['tm_v'], bn, t_len)

    q3 = q.reshape(bn, t_len, ATT_WIDTH)
    kaug3 = kaug.reshape(bn, t_len, 2 * ATT_WIDTH)
    tq, tk = cfg['tq'], cfg['tk']
    tpad = tq - t_len if past else 0
    if past:
        kaug_past = _kaug_call(k_past.reshape(bn * past, ATT_WIDTH).astype(F32), cum_past.reshape(bn * past, LANES),
                               cfg['tm_proj']).reshape(bn, past, 2 * ATT_WIDTH)
        kaug3 = jnp.concatenate([kaug_past, jnp.pad(kaug3, ((0, 0), (0, tpad), (0, 0)))], axis=1)
        vt_past = v_past.astype(BF16).transpose(0, 2, 3, 1)
        ones_rows = jnp.zeros((bn, ATT_HEADS, VT_ROWS - ATT_HEAD_DIM, past), BF16).at[:, :, 0, :].set(1.0)
        vt4 = jnp.concatenate([jnp.concatenate([vt_past, ones_rows], axis=2),
                               jnp.pad(vt4, ((0, 0), (0, 0), (0, 0), (0, tpad)))], axis=3)
        q3 = jnp.pad(q3, ((0, 0), (0, tpad), (0, 0)))
    js = _attn_block_starts(cum_all, p['q_norm_w'], p['k_norm_w'], tq, tk, past, (t_len + tpad) // tq)
    yf = _attn_call(js, q3, kaug3, vt4, tq, tk, cfg['n_diag'], past)[:, :t_len].reshape(m, ATT_WIDTH)

    zx3 = zx.reshape(bn, t_len, SSD_INNER + SSD_CONV_DIM)
    cs8 = jnp.pad(mconv_buf.astype(F32), ((0, 0), (SUBLANES_F32 - (SSD_CONV_W - 1), 0), (0, 0)))
    hpg = SSD_HEADS_PER_GROUP
    h0t = ssm_h0.astype(F32).reshape(bn, SSD_GROUPS, hpg, SSD_HEAD_DIM, SSD_STATE)
    h0t = h0t.transpose(0, 1, 4, 2, 3).reshape(bn, SSD_GROUPS, SSD_STATE, SSD_GROUP_W)
    ymn, h_t = _ssd_call(zx3, dtf3, cs8, h0t, p['m_conv_w'], p['m_conv_b'].reshape(1, SSD_CONV_DIM),
                         w['a_log'], w['d_row'], p['m_norm_w'].reshape(1, SSD_INNER), cfg['tb'], cfg['q'])
    new_h = h_t.reshape(bn, SSD_GROUPS, SSD_STATE, hpg, SSD_HEAD_DIM).transpose(0, 1, 3, 4, 2)
    new_h = new_h.reshape(bn, SSD_HEADS, SSD_HEAD_DIM, SSD_STATE)
    new_mconv = zx3[:, t_len - (SSD_CONV_W - 1):, SSD_INNER:]

    mix = _merge_call(h, ymn.reshape(m, SSD_INNER), yf, w['gates'], w['pm'], w['pf'], cfg['tm_proj'], 512)
    x1, h2 = _outproj_call(mix, w['out'], x2, mod3, p['norm2_w'], cfg['tm_norm'], t_len)

    u_init = jnp.pad(_pad_ff(ffn_buf.astype(F32), 2), ((0, 0), (FFN_HALO - (FFN_CONV_W - 1), 0), (0, 0)))
    y = _ffn_call(h2, u_init, w['up'], w['ffn_cw'], w['ffn_cb'], w['down'], x1, mod3, cfg['tm_ffn'], 512, t_len)

    tail_rows = h2.reshape(bn, t_len, D_MODEL)[:, t_len - (FFN_CONV_W - 1):].reshape(bn * (FFN_CONV_W - 1), D_MODEL)
    n_tail = tail_rows.shape[0]
    tail_pad = -n_tail % SUBLANES_BF16
    u_tail = _mm_call(jnp.pad(tail_rows, ((0, tail_pad), (0, 0))), w['up'], n_tail + tail_pad, 1408, "ffn_state")
    u_tail = u_tail[:n_tail]
    new_ffn = jnp.concatenate([u_tail[:, :D_FF], u_tail[:, D_FF_PAD:D_FF_PAD + D_FF]], axis=1)
    new_ffn = new_ffn.reshape(bn, FFN_CONV_W - 1, 2 * D_FF)

    return (y.reshape(bn, t_len, D_MODEL), k_new.reshape(bn, t_len, ATT_HEADS, ATT_HEAD_DIM),
            v_new.reshape(bn, t_len, ATT_HEADS, ATT_HEAD_DIM), logf, new_h, new_mconv, new_ffn)


PROMPT_CFG = dict(tm_norm=512, tm_proj=512, tm_v=512, tm_zx=1024, tq=1024, tk=512, n_diag=1, tb=1024, q=128, tm_ffn=512)
SAMPLE_CFG = dict(tm_norm=64, tm_proj=512, tm_v=64, tm_zx=512, tq=128, tk=512, n_diag=1, tb=64, q=64, tm_ffn=64)


def kernel(x_prompt, x_sample, c_prompt, c_sample, cache_fox_k, cache_fox_v, cache_fox_logf, state_ssm, state_mamba_conv, state_ffn_conv, norm1_w, norm2_w, w_ada, b_ada, w_in, m_conv_w, m_conv_b, m_dt_bias, m_a_log, m_d, m_norm_w, f_bias, q_norm_w, k_norm_w, w_proj_m, w_proj_f, w_out, w_up, ffn_conv_w, ffn_conv_b, w_down):
    bp, bs = x_prompt.shape[0], x_sample.shape[0]
    depth = w_in.shape[0]
    xp, xs = x_prompt, x_sample
    c_all = jnp.concatenate([c_prompt, c_sample], axis=0)
    c_all = jnp.pad(c_all, ((0, -(bp + bs) % SUBLANES_BF16), (0, 0)))
    outs_p = [[] for _ in range(6)]
    outs_s = [[] for _ in range(6)]
    for l in range(depth):
        p = {
            'norm1_w': norm1_w[l], 'norm2_w': norm2_w[l], 'w_in': w_in[l], 'm_conv_w': m_conv_w[l],
            'm_conv_b': m_conv_b[l], 'm_dt_bias': m_dt_bias[l], 'm_a_log': m_a_log[l], 'm_d': m_d[l],
            'm_norm_w': m_norm_w[l], 'f_bias': f_bias[l], 'q_norm_w': q_norm_w[l], 'k_norm_w': k_norm_w[l],
            'w_proj_m': w_proj_m[l], 'w_proj_f': w_proj_f[l], 'w_out': w_out[l], 'w_up': w_up[l],
            'ffn_conv_w': ffn_conv_w[l], 'ffn_conv_b': ffn_conv_b[l], 'w_down': w_down[l],
        }
        w = _prep_weights(p)
        mod = _mod_call(c_all, w_ada[l], b_ada[l])
        dt = xp.dtype
        res_p = _layer(xp, mod[:bp], None, None, None,
                       jnp.zeros((bp, SSD_HEADS, SSD_HEAD_DIM, SSD_STATE), dt),
                       jnp.zeros((bp, SSD_CONV_W - 1, SSD_CONV_DIM), dt),
                       jnp.zeros((bp, FFN_CONV_W - 1, 2 * D_FF), dt), p, w, PROMPT_CFG)
        res_s = _layer(xs, mod[bp:bp + bs], cache_fox_k[l], cache_fox_v[l], cache_fox_logf[l], state_ssm[l],
                       state_mamba_conv[l], state_ffn_conv[l], p, w, SAMPLE_CFG)
        xp, xs = res_p[0], res_s[0]
        for i in range(6):
            outs_p[i].append(res_p[i + 1])
            outs_s[i].append(res_s[i + 1])
    k_p, v_p, lf_p, ssm_p, mc_p, fc_p = [jnp.stack(o, axis=0) for o in outs_p]
    k_s, v_s, lf_s, ssm_s, mc_s, fc_s = [jnp.stack(o, axis=0) for o in outs_s]
    return (xp, xs, k_p, v_p, lf_p, ssm_p, mc_p, fc_p, k_s, v_s, lf_s, ssm_s, mc_s, fc_s)
```

```python
import functools

import jax
import jax.numpy as jnp
from jax import lax
from jax.experimental import pallas as pl
from jax.experimental.pallas import tpu as pltpu

F32, BF16 = jnp.float32, jnp.bfloat16

D_MODEL = 2048
EPS = 1e-6
N_MOD = 6
SSD_INNER = 4096
SSD_HEADS = 64
SSD_HEAD_DIM = 64
SSD_GROUPS = 8
SSD_STATE = 128
SSD_CONV_W = 4
SSD_CONV_DIM = SSD_INNER + 2 * SSD_GROUPS * SSD_STATE
SSD_GROUP_W = SSD_INNER // SSD_GROUPS
SSD_HEADS_PER_GROUP = SSD_HEADS // SSD_GROUPS
ATT_HEADS = 16
ATT_HEAD_DIM = 128
ATT_WIDTH = ATT_HEADS * ATT_HEAD_DIM
ATT_SCALE = ATT_HEAD_DIM ** -0.5
D_FF = 5504
D_FF_PAD = 5632
FFN_CONV_W = 3
LANES = 128
SUBLANES_F32 = 8
SUBLANES_BF16 = 16
VMEM_LIMIT_BYTES = 56 * 1024 * 1024

DT_LANES = SSD_HEADS
LOGF_LANE0 = SSD_HEADS
NEG_BIG = -1e30
LOG2E = 1.4426950408889634
VT_ROWS = ATT_HEAD_DIM + SUBLANES_BF16
EXP_UNDERFLOW = 104.0


def _cparams(semantics):
    return pltpu.CompilerParams(dimension_semantics=semantics, vmem_limit_bytes=VMEM_LIMIT_BYTES)


def _silu(v):
    return v * jax.nn.sigmoid(v)


def _split3(v):
    hi = v.astype(BF16)
    r1 = v - hi.astype(F32)
    mid = r1.astype(BF16)
    lo = (r1 - mid.astype(F32)).astype(BF16)
    return hi, mid, lo


def _mod_kernel(c_ref, w_ref, b_ref, o_ref):
    c = c_ref[...]
    o_ref[...] = jnp.dot(_silu(c).astype(BF16), w_ref[...].astype(BF16),
                         preferred_element_type=F32) + b_ref[...]


def _mod_call(c_all, w_ada, b_ada):
    rows, n = c_all.shape[0], w_ada.shape[1]
    tn = 1024
    return pl.pallas_call(
        _mod_kernel,
        grid=(n // tn,),
        in_specs=[pl.BlockSpec((rows, D_MODEL), lambda j: (0, 0)),
                  pl.BlockSpec((D_MODEL, tn), lambda j: (0, j)),
                  pl.BlockSpec((1, tn), lambda j: (0, j))],
        out_specs=pl.BlockSpec((rows, tn), lambda j: (0, j)),
        out_shape=jax.ShapeDtypeStruct((rows, n), F32),
        compiler_params=_cparams(("arbitrary",)),
        name="ada_mod",
    )(c_all, w_ada, b_ada.reshape(1, n))


def _modulated_rms(x, nw, sc, sh):
    r = lax.rsqrt(jnp.mean(x * x, axis=-1, keepdims=True) + EPS)
    return (x * r) * nw * (1.0 + sc) + sh


def _prenorm_kernel(x_ref, nw_ref, sh_ref, sc_ref, o_ref):
    o_ref[...] = _modulated_rms(x_ref[...], nw_ref[...], sc_ref[...], sh_ref[...]).astype(BF16)


def _prenorm_call(x2, mod3, norm_w, seg_shift, seg_scale, tm, rows_per_batch):
    m = x2.shape[0]
    tpb = rows_per_batch // tm
    return pl.pallas_call(
        _prenorm_kernel,
        grid=(m // tm,),
        in_specs=[pl.BlockSpec((tm, D_MODEL), lambda i: (i, 0)),
                  pl.BlockSpec((1, D_MODEL), lambda i: (0, 0)),
                  pl.BlockSpec((None, 1, D_MODEL), lambda i: (i // tpb, 0, seg_shift)),
                  pl.BlockSpec((None, 1, D_MODEL), lambda i: (i // tpb, 0, seg_scale))],
        out_specs=pl.BlockSpec((tm, D_MODEL), lambda i: (i, 0)),
        out_shape=jax.ShapeDtypeStruct((m, D_MODEL), BF16),
        compiler_params=_cparams(("arbitrary",)),
        name="prenorm",
    )(x2, norm_w.reshape(1, D_MODEL), mod3, mod3)


def _mm_kernel(a_ref, w_ref, o_ref):
    o_ref[...] = jnp.dot(a_ref[...], w_ref[...], preferred_element_type=F32)


def _mm_call(a, w, tm, tn, name):
    m, k = a.shape
    n = w.shape[1]
    return pl.pallas_call(
        _mm_kernel,
        grid=(m // tm, n // tn),
        in_specs=[pl.BlockSpec((tm, k), lambda i, j: (i, 0)),
                  pl.BlockSpec((k, tn), lambda i, j: (0, j))],
        out_specs=pl.BlockSpec((tm, tn), lambda i, j: (i, j)),
        out_shape=jax.ShapeDtypeStruct((m, n), F32),
        compiler_params=_cparams(("arbitrary", "arbitrary")),
        name=name,
    )(a, w)


def _dt_logf(h, w_small, b_small):
    v = jnp.dot(h, w_small, preferred_element_type=F32) + b_small
    lane = lax.broadcasted_iota(jnp.int32, v.shape, 1)
    tail = jnp.log1p(jnp.exp(-jnp.abs(v)))
    softplus = jnp.maximum(v, 0.0) + tail
    log_sigmoid = jnp.minimum(v, 0.0) - tail
    return jnp.where(lane < DT_LANES, softplus, jnp.where(lane < LOGF_LANE0 + ATT_HEADS, log_sigmoid, 0.0))


CUM_ROWS = 64


def _tri(n, lower):
    r = lax.broadcasted_iota(jnp.int32, (n, n), 0)
    c = lax.broadcasted_iota(jnp.int32, (n, n), 1)
    return (c <= r) if lower else (r <= c)


def _cumsum_rows(x, tril3):
    hi, mid, lo = _split3(x)
    return jnp.dot(tril3, jnp.concatenate([hi, mid, lo], axis=0), preferred_element_type=F32)


def _cumsum_kernel(x_ref, o_ref, *, n_chunks):
    tril = _tri(CUM_ROWS, True).astype(BF16)
    tril3 = jnp.concatenate([tril, tril, tril], axis=1)

    def body(c, carry):
        s = pl.multiple_of(c * CUM_ROWS, CUM_ROWS)
        y = _cumsum_rows(x_ref[pl.ds(s, CUM_ROWS), :], tril3) + carry
        o_ref[pl.ds(s, CUM_ROWS), :] = y
        return y[CUM_ROWS - 1:CUM_ROWS, :]

    lax.fori_loop(0, n_chunks, body, jnp.zeros((1, LANES), F32))


def _cumsum_call(x3):
    bn, length, _ = x3.shape
    return pl.pallas_call(
        functools.partial(_cumsum_kernel, n_chunks=length // CUM_ROWS),
        grid=(bn,),
        in_specs=[pl.BlockSpec((None, length, LANES), lambda b: (b, 0, 0))],
        out_specs=pl.BlockSpec((None, length, LANES), lambda b: (b, 0, 0)),
        out_shape=jax.ShapeDtypeStruct(x3.shape, F32),
        compiler_params=_cparams(("arbitrary",)),
        name="logf_cumsum",
    )(x3)


def _head_rms(blk, w_row):
    ms = jnp.mean(blk * blk, axis=-1, keepdims=True)
    return blk * lax.rsqrt(ms + EPS) * w_row


def _key_bias_columns(cum, head, rows):
    lane = lax.broadcasted_iota(jnp.int32, (rows, LANES), 1)
    c = jnp.broadcast_to(cum[:, LOGF_LANE0 + head:LOGF_LANE0 + head + 1] * (-LOG2E), (rows, LANES))
    hi, mid, lo = _split3(c)
    return jnp.where(lane == 0, hi.astype(F32),
                     jnp.where(lane == 1, mid.astype(F32),
                               jnp.where(lane == 2, lo.astype(F32), 0.0)))


def _q_kernel(h_ref, w_ref, nw_ref, ws_ref, bs_ref, q_ref, dtf_ref):
    dtf_ref[...] = _dt_logf(h_ref[...], ws_ref[...], bs_ref[...])
    acc = jnp.dot(h_ref[...], w_ref[...], preferred_element_type=F32)
    for hh in range(ATT_HEADS):
        sl = slice(hh * ATT_HEAD_DIM, (hh + 1) * ATT_HEAD_DIM)
        q_ref[:, sl] = (_head_rms(acc[:, sl], nw_ref[...]) * (ATT_SCALE * LOG2E)).astype(BF16)


def _k_kernel(h_ref, w_ref, nw_ref, cum_ref, k_ref, kaug_ref):
    acc = jnp.dot(h_ref[...], w_ref[...], preferred_element_type=F32)
    rows = acc.shape[0]
    cum = cum_ref[...]
    for hh in range(ATT_HEADS):
        sl = slice(hh * ATT_HEAD_DIM, (hh + 1) * ATT_HEAD_DIM)
        kn = _head_rms(acc[:, sl], nw_ref[...])
        k_ref[:, sl] = kn
        kaug_ref[:, 2 * hh * LANES:(2 * hh + 1) * LANES] = kn.astype(BF16)
        kaug_ref[:, (2 * hh + 1) * LANES:(2 * hh + 2) * LANES] = _key_bias_columns(cum, hh, rows).astype(BF16)


def _kaug_kernel(k_ref, cum_ref, kaug_ref):
    rows = k_ref.shape[0]
    cum = cum_ref[...]
    for hh in range(ATT_HEADS):
        sl = slice(hh * ATT_HEAD_DIM, (hh + 1) * ATT_HEAD_DIM)
        kaug_ref[:, 2 * hh * LANES:(2 * hh + 1) * LANES] = k_ref[:, sl].astype(BF16)
        kaug_ref[:, (2 * hh + 1) * LANES:(2 * hh + 2) * LANES] = _key_bias_columns(cum, hh, rows).astype(BF16)


def _v_kernel(h_ref, w_ref, v_ref, vt_ref):
    acc = jnp.dot(h_ref[...], w_ref[...], preferred_element_type=F32)
    tm = acc.shape[0]
    v_ref[...] = acc
    row = lax.broadcasted_iota(jnp.int32, (VT_ROWS - ATT_HEAD_DIM, tm), 0)
    ones_rows = jnp.where(row == 0, 1.0, 0.0).astype(BF16)
    for hh in range(ATT_HEADS):
        vt_ref[hh, 0:ATT_HEAD_DIM, :] = acc[:, hh * ATT_HEAD_DIM:(hh + 1) * ATT_HEAD_DIM].T.astype(BF16)
        vt_ref[hh, ATT_HEAD_DIM:VT_ROWS, :] = ones_rows


def _row_spec(tm, n):
    return pl.BlockSpec((tm, n), lambda i: (i, 0))


def _const_spec(r, n):
    return pl.BlockSpec((r, n), lambda i: (0, 0))


def _q_call(h, w, nw, w_small, b_small, tm):
    m = h.shape[0]
    return pl.pallas_call(
        _q_kernel, grid=(m // tm,),
        in_specs=[_row_spec(tm, D_MODEL), _const_spec(D_MODEL, ATT_WIDTH), _const_spec(1, ATT_HEAD_DIM),
                  _const_spec(D_MODEL, LANES), _const_spec(1, LANES)],
        out_specs=[_row_spec(tm, ATT_WIDTH), _row_spec(tm, LANES)],
        out_shape=[jax.ShapeDtypeStruct((m, ATT_WIDTH), BF16), jax.ShapeDtypeStruct((m, LANES), F32)],
        compiler_params=_cparams(("arbitrary",)), name="q_proj",
    )(h, w, nw, w_small, b_small)


def _k_call(h, w, nw, cum2, tm):
    m = h.shape[0]
    return pl.pallas_call(
        _k_kernel, grid=(m // tm,),
        in_specs=[_row_spec(tm, D_MODEL), _const_spec(D_MODEL, ATT_WIDTH), _const_spec(1, ATT_HEAD_DIM),
                  _row_spec(tm, LANES)],
        out_specs=[_row_spec(tm, ATT_WIDTH), _row_spec(tm, 2 * ATT_WIDTH)],
        out_shape=[jax.ShapeDtypeStruct((m, ATT_WIDTH), F32), jax.ShapeDtypeStruct((m, 2 * ATT_WIDTH), BF16)],
        compiler_params=_cparams(("arbitrary",)), name="k_proj",
    )(h, w, nw, cum2)


def _kaug_call(k2, cum2, tm):
    m = k2.shape[0]
    return pl.pallas_call(
        _kaug_kernel, grid=(m // tm,),
        in_specs=[_row_spec(tm, ATT_WIDTH), _row_spec(tm, LANES)],
        out_specs=_row_spec(tm, 2 * ATT_WIDTH),
        out_shape=jax.ShapeDtypeStruct((m, 2 * ATT_WIDTH), BF16),
        compiler_params=_cparams(("arbitrary",)), name="k_cache_aug",
    )(k2, cum2)


def _v_call(h, w, tm, bn, t_len):
    m = h.shape[0]
    tpb = t_len // tm
    return pl.pallas_call(
        _v_kernel, grid=(m // tm,),
        in_specs=[_row_spec(tm, D_MODEL), _const_spec(D_MODEL, ATT_WIDTH)],
        out_specs=[_row_spec(tm, ATT_WIDTH),
                   pl.BlockSpec((None, ATT_HEADS, VT_ROWS, tm), lambda i: (i // tpb, 0, 0, i % tpb))],
        out_shape=[jax.ShapeDtypeStruct((m, ATT_WIDTH), F32),
                   jax.ShapeDtypeStruct((bn, ATT_HEADS, VT_ROWS, t_len), BF16)],
        compiler_params=_cparams(("arbitrary",)), name="v_proj",
    )(h, w)


def _attn_kernel(js_ref, q_ref, k_ref, vt_ref, o_ref, acc_ref, s_ref, *, tq, tk, n_diag, past):
    b, h, i = pl.program_id(0), pl.program_id(1), pl.program_id(2)
    nq = pl.num_programs(2)
    lane = lax.broadcasted_iota(jnp.int32, (tq, LANES), 1)
    ones = jnp.where(lane < 3, 1.0, 0.0).astype(BF16)
    qa = jnp.concatenate([q_ref[...], ones], axis=1)
    acc_ref[...] = jnp.zeros(acc_ref.shape, F32)

    def block(kb, vt_blk, m, c0, masked):
        n = kb.shape[0]
        st = lax.dot_general(kb, qa[c0:, :], (((1,), (1,)), ((), ())), preferred_element_type=F32)
        if masked:
            r = lax.broadcasted_iota(jnp.int32, st.shape, 0)
            c = lax.broadcasted_iota(jnp.int32, st.shape, 1)
            st = jnp.where(r <= c, st, NEG_BIG)
        s_ref[0:n, c0:] = st
        m_new = jnp.maximum(m, jnp.max(st, axis=0, keepdims=True))
        alpha = jnp.exp2(m - m_new)
        p = jnp.exp2(s_ref[0:n, c0:] - m_new).astype(BF16)
        pv = jnp.dot(vt_blk, p, preferred_element_type=F32)
        acc_ref[:, c0:] = alpha * acc_ref[:, c0:] + pv
        return m_new

    def body(j, m):
        s = pl.multiple_of(j * tk, tk)
        return block(k_ref[pl.ds(s, tk), :], vt_ref[:, pl.ds(s, tk)], m, 0, False)

    j_start = js_ref[(b * ATT_HEADS + h) * nq + i]
    m = lax.fori_loop(j_start, past // tk + i * (tq // tk), body, jnp.full((1, tq), NEG_BIG, F32))

    hw = tq // n_diag
    for d in range(n_diag):
        d0 = pl.multiple_of(past + i * tq + d * hw, hw)
        m_hi = block(k_ref[pl.ds(d0, hw), :], vt_ref[:, pl.ds(d0, hw)], m[:, d * hw:], d * hw, True)
        m = m_hi if d == 0 else jnp.concatenate([m[:, :d * hw], m_hi], axis=1)

    acc = acc_ref[...]
    o = acc[0:ATT_HEAD_DIM, :] * (1.0 / acc[ATT_HEAD_DIM:ATT_HEAD_DIM + 1, :])
    o_ref[...] = o.T.astype(BF16)


def _attn_block_starts(cum_all, q_norm_w, k_norm_w, tq, tk, past, nq):
    bn = cum_all.shape[0]
    bound = ATT_HEAD_DIM * jnp.max(jnp.abs(q_norm_w)) * jnp.max(jnp.abs(k_norm_w)) * ATT_SCALE
    thr = EXP_UNDERFLOW + 2.0 * bound
    c = cum_all[:, :, LOGF_LANE0:LOGF_LANE0 + ATT_HEADS]
    n_off = past // tk + jnp.arange(nq) * (tq // tk)
    nb = past // tk + (nq - 1) * (tq // tk)
    if nb == 0:
        return jnp.zeros((bn * ATT_HEADS * nq,), jnp.int32)
    t_real = c.shape[1] - past
    cq = jnp.pad(c[:, past:], ((0, 0), (0, nq * tq - t_real), (0, 0)), constant_values=-jnp.inf)
    cq_max = jnp.max(cq.reshape(bn, nq, tq, ATT_HEADS), axis=2)
    ck_min = jnp.min(c[:, :nb * tk].reshape(bn, nb, tk, ATT_HEADS), axis=2)
    gap = cq_max[:, :, None, :] - ck_min[:, None, :, :]
    jj = jnp.arange(nb)[None, None, :, None]
    cand = jnp.where((gap >= -thr) & (jj < n_off[None, :, None, None]), jj, n_off[None, :, None, None])
    js = jnp.min(cand, axis=2)
    return js.transpose(0, 2, 1).reshape(-1).astype(jnp.int32)


def _attn_call(js, q3, kaug3, vt4, tq, tk, n_diag, past):
    bn, t_q, _ = q3.shape
    t_k = kaug3.shape[1]
    grid_spec = pltpu.PrefetchScalarGridSpec(
        num_scalar_prefetch=1,
        grid=(bn, ATT_HEADS, t_q // tq),
        in_specs=[pl.BlockSpec((None, tq, ATT_HEAD_DIM), lambda b, h, i, js_ref: (b, i, h)),
                  pl.BlockSpec((None, t_k, 2 * ATT_HEAD_DIM), lambda b, h, i, js_ref: (b, 0, h)),
                  pl.BlockSpec((None, None, VT_ROWS, t_k), lambda b, h, i, js_ref: (b, h, 0, 0))],
        out_specs=pl.BlockSpec((None, tq, ATT_HEAD_DIM), lambda b, h, i, js_ref: (b, i, h)),
        scratch_shapes=[pltpu.VMEM((VT_ROWS, tq), F32), pltpu.VMEM((max(tk, tq // n_diag), tq), F32)])
    return pl.pallas_call(
        functools.partial(_attn_kernel, tq=tq, tk=tk, n_diag=n_diag, past=past),
        grid_spec=grid_spec,
        out_shape=jax.ShapeDtypeStruct((bn, t_q, ATT_WIDTH), BF16),
        compiler_params=_cparams(("arbitrary", "arbitrary", "arbitrary")),
        name="fox_attention",
    )(js, q3, kaug3, vt4)


def _ssd_kernel(x_ref, b_ref, c_ref, z_ref, dt_ref, csx_ref, csb_ref, csc_ref, h0_ref,
                cwx_ref, cwb_ref, cwc_ref, cbx_ref, cbb_ref, cbc_ref, alog_ref, d_ref, nw_ref,
                y_ref, hout_ref,
                ext_x, ext_b, ext_c, h_s, *, tb, q, n_chunks, n_tblocks):
    g = pl.program_id(1)
    t = pl.program_id(2)
    halo = SUBLANES_F32
    n_slabs = SSD_GROUP_W // LANES

    @pl.when(t == 0)
    def _():
        ext_x[0:halo, :] = csx_ref[...]
        ext_b[0:halo, :] = csb_ref[...]
        ext_c[0:halo, :] = csc_ref[...]
        h_s[...] = h0_ref[...]

    ext_x[halo:halo + tb, :] = x_ref[...]
    ext_b[halo:halo + tb, :] = b_ref[...]
    ext_c[halo:halo + tb, :] = c_ref[...]

    def conv(ext, cw_ref, cb_ref, s):
        e = ext[pl.ds(s, q + halo), :]
        y = cb_ref[...]
        for j in range(SSD_CONV_W):
            back = SSD_CONV_W - 1 - j
            win = e if back == 0 else pltpu.roll(e, back, 0)
            y = y + win[halo:halo + q, :] * cw_ref[j:j + 1, :]
        return _silu(y)

    lane_row = lax.broadcasted_iota(jnp.int32, (1, LANES), 1)
    a_row = jnp.where(lane_row < DT_LANES, -jnp.exp(alog_ref[...]), 0.0)
    shift = lax.rem(LANES - SSD_HEADS_PER_GROUP * g, LANES)
    tril_mask = _tri(q, True)
    tril = tril_mask.astype(BF16)
    tril3 = jnp.concatenate([tril, tril, tril], axis=1)
    lo_half = lax.broadcasted_iota(jnp.int32, (q, LANES), 1) < SSD_HEAD_DIM

    def chunk(ci, carry):
        s = pl.multiple_of(ci * q, q)
        dt_all = dt_ref[pl.ds(s, q), :]
        cum_all = _cumsum_rows(dt_all * a_row, tril3)
        cum_g = pltpu.roll(cum_all, shift, 1)
        dt_g = pltpu.roll(dt_all, shift, 1)
        cum_t = cum_g.T

        xc = conv(ext_x, cwx_ref, cbx_ref, s)
        bc = conv(ext_b, cwb_ref, cbb_ref, s).astype(BF16)
        cc = conv(ext_c, cwc_ref, cbc_ref, s).astype(BF16)
        cb = lax.dot_general(cc, bc, (((1,), (1,)), ((), ())), preferred_element_type=F32)
        h_t = h_s[...]
        y_inter = jnp.dot(cc, h_t.astype(BF16), preferred_element_type=F32)

        y_slabs, xd_slabs, etot_slabs = [], [], []
        for k in range(n_slabs):
            h0i, h1i = 2 * k, 2 * k + 1
            xck = xc[:, k * LANES:(k + 1) * LANES]
            cum_b = {hh: jnp.broadcast_to(cum_g[:, hh:hh + 1], (q, LANES)) for hh in (h0i, h1i)}
            dt_b = {hh: jnp.broadcast_to(dt_g[:, hh:hh + 1], (q, LANES)) for hh in (h0i, h1i)}
            cum_pair = jnp.where(lo_half, cum_b[h0i], cum_b[h1i])
            last_pair = cum_pair[q - 1:q, :]
            xdt = xck * jnp.where(lo_half, dt_b[h0i], dt_b[h1i])
            y_slab = y_inter[:, k * LANES:(k + 1) * LANES] * jnp.exp(cum_pair) + xck * d_ref[:, k * LANES:(k + 1) * LANES]
            for hi_, sel in ((h0i, lo_half), (h1i, jnp.logical_not(lo_half))):
                seg = cum_b[hi_][:, 0:q] - cum_t[hi_:hi_ + 1, :]
                lmat = jnp.exp(jnp.where(tril_mask, seg, -jnp.inf))
                mh = (cb * lmat).astype(BF16)
                rhs = jnp.where(sel, xdt, 0.0).astype(BF16)
                y_slab = y_slab + jnp.dot(mh, rhs, preferred_element_type=F32)
            y_slabs.append(y_slab)
            xd_slabs.append((xdt * jnp.exp(last_pair - cum_pair)).astype(BF16))
            etot_slabs.append(jnp.exp(last_pair))

        xd = jnp.concatenate(xd_slabs, axis=1)
        h_s[...] = h_t * jnp.concatenate(etot_slabs, axis=1) + lax.dot_general(
            bc, xd, (((0,), (0,)), ((), ())), preferred_element_type=F32)

        y = jnp.concatenate(y_slabs, axis=1) * _silu(z_ref[pl.ds(s, q), :])
        y = y * lax.rsqrt(jnp.mean(y * y, axis=-1, keepdims=True) + EPS) * nw_ref[...]
        y_ref[pl.ds(s, q), :] = y.astype(BF16)
        return carry

    lax.fori_loop(0, n_chunks, chunk, 0, unroll=min(2, n_chunks))

    for ext in (ext_x, ext_b, ext_c):
        ext[0:halo, :] = ext[tb:tb + halo, :]

    @pl.when(t == n_tblocks - 1)
    def _():
        hout_ref[...] = h_s[...]


def _ssd_call(zx3, dtf3, cs8, h0t, conv_w, conv_b, a_log_row, d_row, norm_w_row, tb, q):
    bn, t_len, _ = zx3.shape
    n_tblocks = t_len // tb
    gw, st = SSD_GROUP_W, SSD_STATE
    zx_x0, zx_b0, zx_c0 = SSD_INNER // gw, 2 * SSD_INNER // st, 2 * SSD_INNER // st + SSD_GROUPS
    cv_b0, cv_c0 = SSD_INNER // st, SSD_INNER // st + SSD_GROUPS

    def rows(width, col0):
        return pl.BlockSpec((None, tb, width), lambda b, g, t: (b, t, col0 + g))

    def per_group(r, width, col0):
        return pl.BlockSpec((r, width), lambda b, g, t: (0, col0 + g))

    def tail(width, col0):
        return pl.BlockSpec((None, SUBLANES_F32, width), lambda b, g, t: (b, 0, col0 + g))

    in_specs = [
        rows(gw, zx_x0), rows(st, zx_b0), rows(st, zx_c0), rows(gw, 0),
        pl.BlockSpec((None, tb, LANES), lambda b, g, t: (b, t, 0)),
        tail(gw, 0), tail(st, cv_b0), tail(st, cv_c0),
        pl.BlockSpec((None, None, st, gw), lambda b, g, t: (b, g, 0, 0)),
        per_group(SSD_CONV_W, gw, 0), per_group(SSD_CONV_W, st, cv_b0), per_group(SSD_CONV_W, st, cv_c0),
        per_group(1, gw, 0), per_group(1, st, cv_b0), per_group(1, st, cv_c0),
        pl.BlockSpec((1, LANES), lambda b, g, t: (0, 0)),
        per_group(1, gw, 0), per_group(1, gw, 0),
    ]
    out_specs = [pl.BlockSpec((None, tb, gw), lambda b, g, t: (b, t, g)),
                 pl.BlockSpec((None, None, st, gw), lambda b, g, t: (b, g, 0, 0))]
    out_shape = [jax.ShapeDtypeStruct((bn, t_len, SSD_INNER), BF16),
                 jax.ShapeDtypeStruct((bn, SSD_GROUPS, st, gw), F32)]
    scratch = [pltpu.VMEM((tb + SUBLANES_F32, gw), F32), pltpu.VMEM((tb + SUBLANES_F32, st), F32),
               pltpu.VMEM((tb + SUBLANES_F32, st), F32), pltpu.VMEM((st, gw), F32)]
    return pl.pallas_call(
        functools.partial(_ssd_kernel, tb=tb, q=q, n_chunks=tb // q, n_tblocks=n_tblocks),
        grid=(bn, SSD_GROUPS, n_tblocks),
        in_specs=in_specs, out_specs=out_specs, out_shape=out_shape, scratch_shapes=scratch,
        compiler_params=_cparams(("arbitrary", "arbitrary", "arbitrary")),
        name="ssd_scan",
    )(zx3, zx3, zx3, zx3, dtf3, cs8, cs8, cs8, h0t, conv_w, conv_w, conv_w, conv_b, conv_b, conv_b,
      a_log_row, d_row, norm_w_row)


def _merge_kernel(h_ref, ym_ref, yf_ref, wgm_ref, wgf_ref, wm_ref, wf_ref, o_ref):
    h = h_ref[...]
    gm = jax.nn.sigmoid(jnp.dot(h, wgm_ref[...], preferred_element_type=F32))
    gf = jax.nn.sigmoid(jnp.dot(h, wgf_ref[...], preferred_element_type=F32))
    a = jnp.dot(ym_ref[...], wm_ref[...], preferred_element_type=F32)
    b = jnp.dot(yf_ref[...], wf_ref[...], preferred_element_type=F32)
    o_ref[...] = (gm * a + gf * b).astype(BF16)


def _merge_call(h, ymn, yf, w_gates, w_pm, w_pf, tm, tn):
    m = h.shape[0]
    n_tiles = D_MODEL // tn
    return pl.pallas_call(
        _merge_kernel, grid=(m // tm, n_tiles),
        in_specs=[pl.BlockSpec((tm, D_MODEL), lambda i, j: (i, 0)),
                  pl.BlockSpec((tm, SSD_INNER), lambda i, j: (i, 0)),
                  pl.BlockSpec((tm, ATT_WIDTH), lambda i, j: (i, 0)),
                  pl.BlockSpec((D_MODEL, tn), lambda i, j: (0, j)),
                  pl.BlockSpec((D_MODEL, tn), lambda i, j: (0, n_tiles + j)),
                  pl.BlockSpec((SSD_INNER, tn), lambda i, j: (0, j)),
                  pl.BlockSpec((ATT_WIDTH, tn), lambda i, j: (0, j))],
        out_specs=pl.BlockSpec((tm, tn), lambda i, j: (i, j)),
        out_shape=jax.ShapeDtypeStruct((m, D_MODEL), BF16),
        compiler_params=_cparams(("arbitrary", "arbitrary")), name="gated_merge",
    )(h, ymn, yf, w_gates, w_gates, w_pm, w_pf)


def _outproj_kernel(mix_ref, w_ref, x_ref, g_ref, nw_ref, sh_ref, sc_ref, x1_ref, h2_ref):
    y = jnp.dot(mix_ref[...], w_ref[...], preferred_element_type=F32)
    x1 = x_ref[...] + g_ref[...] * y
    x1_ref[...] = x1
    h2_ref[...] = _modulated_rms(x1, nw_ref[...], sc_ref[...], sh_ref[...]).astype(BF16)


def _outproj_call(mix, w_out, x2, mod3, norm2_w, tm, rows_per_batch):
    m = x2.shape[0]
    tpb = rows_per_batch // tm

    def mod_seg(seg):
        return pl.BlockSpec((None, 1, D_MODEL), lambda i: (i // tpb, 0, seg))

    return pl.pallas_call(
        _outproj_kernel, grid=(m // tm,),
        in_specs=[_row_spec(tm, D_MODEL), _const_spec(D_MODEL, D_MODEL), _row_spec(tm, D_MODEL),
                  mod_seg(2), _const_spec(1, D_MODEL), mod_seg(3), mod_seg(4)],
        out_specs=[_row_spec(tm, D_MODEL), _row_spec(tm, D_MODEL)],
        out_shape=[jax.ShapeDtypeStruct((m, D_MODEL), F32), jax.ShapeDtypeStruct((m, D_MODEL), BF16)],
        compiler_params=_cparams(("arbitrary",)), name="out_proj",
    )(mix, w_out, x2, mod3, norm2_w.reshape(1, D_MODEL), mod3, mod3)


FFN_HALO = SUBLANES_BF16
FFN_SUB_ROWS = 256


def _ffn_kernel(h_ref, halo_ref, uia_ref, uib_ref, wa_ref, wb_ref, cwa_ref, cwb_ref, cba_ref, cbb_ref,
                wd_ref, x_ref, g_ref, o_ref, acc_ref, exta_ref, extb_ref, *, tm, tiles_per_seq, nf):
    i = pl.program_id(0)
    f = pl.program_id(1)
    first = (i % tiles_per_seq) == 0

    @pl.when(f == 0)
    def _():
        acc_ref[...] = jnp.zeros(acc_ref.shape, F32)

    h = h_ref[...]
    hprev = halo_ref[...]

    for ext_ref, w_ref, ui_ref in ((exta_ref, wa_ref, uia_ref), (extb_ref, wb_ref, uib_ref)):
        uh = jnp.dot(hprev, w_ref[...], preferred_element_type=F32)
        ext_ref[0:FFN_HALO, :] = jnp.where(first, ui_ref[...], uh)
        ext_ref[FFN_HALO:FFN_HALO + tm, :] = jnp.dot(h, w_ref[...], preferred_element_type=F32)

    def conv(ext_ref, cw_ref, cb_ref, r0, r1):
        y = cb_ref[...]
        for j in range(FFN_CONV_W):
            off = FFN_HALO - (FFN_CONV_W - 1) + j
            y = y + ext_ref[off + r0:off + r1, :] * cw_ref[j:j + 1, :]
        return y

    ua = _silu(conv(exta_ref, cwa_ref, cba_ref, 0, tm))
    rt = min(tm, FFN_SUB_ROWS)
    for r0 in range(0, tm, rt):
        ub = conv(extb_ref, cwb_ref, cbb_ref, r0, r0 + rt)
        gact = (ua[r0:r0 + rt, :] * ub).astype(BF16)
        acc_ref[r0:r0 + rt, :] += jnp.dot(gact, wd_ref[...], preferred_element_type=F32)

    @pl.when(f == nf - 1)
    def _():
        o_ref[...] = x_ref[...] + g_ref[...] * acc_ref[...]


def _ffn_call(h2, u_init, w_up_p, cw_p, cb_p, w_down_p, x1, mod3, tm, tf, rows_per_seq):
    m = h2.shape[0]
    nf = D_FF_PAD // tf
    tps = rows_per_seq // tm
    halo_blocks = tm // FFN_HALO
    return pl.pallas_call(
        functools.partial(_ffn_kernel, tm=tm, tiles_per_seq=tps, nf=nf),
        grid=(m // tm, nf),
        in_specs=[pl.BlockSpec((tm, D_MODEL), lambda i, f: (i, 0)),
                  pl.BlockSpec((FFN_HALO, D_MODEL), lambda i, f: (jnp.maximum(i * halo_blocks - 1, 0), 0)),
                  pl.BlockSpec((None, FFN_HALO, tf), lambda i, f: (i // tps, 0, f)),
                  pl.BlockSpec((None, FFN_HALO, tf), lambda i, f: (i // tps, 0, nf + f)),
                  pl.BlockSpec((D_MODEL, tf), lambda i, f: (0, f)),
                  pl.BlockSpec((D_MODEL, tf), lambda i, f: (0, nf + f)),
                  pl.BlockSpec((FFN_CONV_W, tf), lambda i, f: (0, f)),
                  pl.BlockSpec((FFN_CONV_W, tf), lambda i, f: (0, nf + f)),
                  pl.BlockSpec((1, tf), lambda i, f: (0, f)),
                  pl.BlockSpec((1, tf), lambda i, f: (0, nf + f)),
                  pl.BlockSpec((tf, D_MODEL), lambda i, f: (f, 0)),
                  pl.BlockSpec((tm, D_MODEL), lambda i, f: (i, 0)),
                  pl.BlockSpec((None, 1, D_MODEL), lambda i, f: (i // tps, 0, 5))],
        out_specs=pl.BlockSpec((tm, D_MODEL), lambda i, f: (i, 0)),
        out_shape=jax.ShapeDtypeStruct((m, D_MODEL), F32),
        scratch_shapes=[pltpu.VMEM((tm, D_MODEL), F32), pltpu.VMEM((tm + FFN_HALO, tf), F32),
                        pltpu.VMEM((tm + FFN_HALO, tf), F32)],
        compiler_params=_cparams(("arbitrary", "arbitrary")), name="conv_mlp",
    )(h2, h2, u_init, u_init, w_up_p, w_up_p, cw_p, cw_p, cb_p, cb_p, w_down_p, x1, mod3)


def _pad_ff(a, axis):
    lo, hi = jnp.split(a, 2, axis=axis)
    pad = [(0, 0)] * a.ndim
    pad[axis] = (0, D_FF_PAD - D_FF)
    return jnp.concatenate([jnp.pad(lo, pad), jnp.pad(hi, pad)], axis=axis)


def _prep_weights(p):
    w_in = p['w_in']
    o_z, o_xbc, o_dt = 0, SSD_INNER, SSD_INNER + SSD_CONV_DIM
    o_q = o_dt + SSD_HEADS
    o_k, o_v = o_q + ATT_WIDTH, o_q + 2 * ATT_WIDTH
    o_f = o_q + 3 * ATT_WIDTH
    o_g = o_f + ATT_HEADS
    pad_small = LANES - SSD_HEADS - ATT_HEADS
    w = {
        'zx': w_in[:, o_z:o_dt].astype(BF16),
        'small': jnp.concatenate([w_in[:, o_dt:o_q], w_in[:, o_f:o_g],
                                  jnp.zeros((D_MODEL, pad_small), F32)], axis=1).astype(BF16),
        'b_small': jnp.concatenate([p['m_dt_bias'], p['f_bias'], jnp.zeros((pad_small,), F32)]).reshape(1, LANES),
        'q': w_in[:, o_q:o_k].astype(BF16),
        'k': w_in[:, o_k:o_v].astype(BF16),
        'v': w_in[:, o_v:o_f].astype(BF16),
        'gates': w_in[:, o_g:o_g + 2 * D_MODEL].astype(BF16),
        'pm': p['w_proj_m'].astype(BF16),
        'pf': p['w_proj_f'].astype(BF16),
        'out': p['w_out'].astype(BF16),
        'up': _pad_ff(p['w_up'], 1).astype(BF16),
        'ffn_cw': _pad_ff(p['ffn_conv_w'], 1),
        'ffn_cb': _pad_ff(p['ffn_conv_b'].reshape(1, 2 * D_FF), 1),
        'down': jnp.pad(p['w_down'], ((0, D_FF_PAD - D_FF), (0, 0))).astype(BF16),
        'a_log': jnp.pad(p['m_a_log'], (0, LANES - SSD_HEADS)).reshape(1, LANES),
        'd_row': jnp.repeat(p['m_d'], SSD_HEAD_DIM).reshape(1, SSD_INNER),
    }
    return w


def _layer(x, mod, k_past, v_past, logf_past, ssm_h0, mconv_buf, ffn_buf, p, w, cfg):
    bn, t_len, _ = x.shape
    m = bn * t_len
    past = 0 if k_past is None else k_past.shape[1]
    mod3 = mod.reshape(bn, 1, N_MOD * D_MODEL)
    x2 = x.reshape(m, D_MODEL)

    h = _prenorm_call(x2, mod3, p['norm1_w'], 0, 1, cfg['tm_norm'], t_len)
    q, dtf = _q_call(h, w['q'], p['q_norm_w'].reshape(1, ATT_HEAD_DIM), w['small'], w['b_small'], cfg['tm_proj'])
    dtf3 = dtf.reshape(bn, t_len, LANES)
    logf = dtf3[:, :, LOGF_LANE0:LOGF_LANE0 + ATT_HEADS]

    if past:
        lf_past = jnp.pad(logf_past.astype(F32), ((0, 0), (0, 0), (LOGF_LANE0, LANES - LOGF_LANE0 - ATT_HEADS)))
        cum_all = _cumsum_call(jnp.concatenate([lf_past, dtf3], axis=1))
        cum_past, cum_new = cum_all[:, :past], cum_all[:, past:]
    else:
        cum_all = cum_new = _cumsum_call(dtf3)

    zx = _mm_call(h, w['zx'], cfg['tm_zx'], 1024, "zx_proj")
    k_new, kaug = _k_call(h, w['k'], p['k_norm_w'].reshape(1, ATT_HEAD_DIM), cum_new.reshape(m, LANES), cfg['tm_proj'])
    v_new, vt4 = _v_call(h, w['v'], cfg['tm_v'], bn, t_len)

    q3 = q.reshape(bn, t_len, ATT_WIDTH)
    kaug3 = kaug.reshape(bn, t_len, 2 * ATT_WIDTH)
    tq, tk = cfg['tq'], cfg['tk']
    tpad = tq - t_len if past else 0
    if past:
        kaug_past = _kaug_call(k_past.reshape(bn * past, ATT_WIDTH).astype(F32), cum_past.reshape(bn * past, LANES),
                               cfg['tm_proj']).reshape(bn, past, 2 * ATT_WIDTH)
        kaug3 = jnp.concatenate([kaug_past, jnp.pad(kaug3, ((0, 0), (0, tpad), (0, 0)))], axis=1)
        vt_past = v_past.astype(BF16).transpose(0, 2, 3, 1)
        ones_rows = jnp.zeros((bn, ATT_HEADS, VT_ROWS - ATT_HEAD_DIM, past), BF16).at[:, :, 0, :].set(1.0)
        vt4 = jnp.concatenate([jnp.concatenate([vt_past, ones_rows], axis=2),
                               jnp.pad(vt4, ((0, 0), (0, 0), (0, 0), (0, tpad)))], axis=3)
        q3 = jnp.pad(q3, ((0, 0), (0, tpad), (0, 0)))
    js = _attn_block_starts(cum_all, p['q_norm_w'], p['k_norm_w'], tq, tk, past, (t_len + tpad) // tq)
    yf = _attn_call(js, q3, kaug3, vt4, tq, tk, cfg['n_diag'], past)[:, :t_len].reshape(m, ATT_WIDTH)

    zx3 = zx.reshape(bn, t_len, SSD_INNER + SSD_CONV_DIM)
    cs8 = jnp.pad(mconv_buf.astype(F32), ((0, 0), (SUBLANES_F32 - (SSD_CONV_W - 1), 0), (0, 0)))
    hpg = SSD_HEADS_PER_GROUP
    h0t = ssm_h0.astype(F32).reshape(bn, SSD_GROUPS, hpg, SSD_HEAD_DIM, SSD_STATE)
    h0t = h0t.transpose(0, 1, 4, 2, 3).reshape(bn, SSD_GROUPS, SSD_STATE, SSD_GROUP_W)
    ymn, h_t = _ssd_call(zx3, dtf3, cs8, h0t, p['m_conv_w'], p['m_conv_b'].reshape(1, SSD_CONV_DIM),
                         w['a_log'], w['d_row'], p['m_norm_w'].reshape(1, SSD_INNER), cfg['tb'], cfg['q'])
    new_h = h_t.reshape(bn, SSD_GROUPS, SSD_STATE, hpg, SSD_HEAD_DIM).transpose(0, 1, 3, 4, 2)
    new_h = new_h.reshape(bn, SSD_HEADS, SSD_HEAD_DIM, SSD_STATE)
    new_mconv = zx3[:, t_len - (SSD_CONV_W - 1):, SSD_INNER:]

    mix = _merge_call(h, ymn.reshape(m, SSD_INNER), yf, w['gates'], w['pm'], w['pf'], cfg['tm_proj'], 512)
    x1, h2 = _outproj_call(mix, w['out'], x2, mod3, p['norm2_w'], cfg['tm_norm'], t_len)

    u_init = jnp.pad(_pad_ff(ffn_buf.astype(F32), 2), ((0, 0), (FFN_HALO - (FFN_CONV_W - 1), 0), (0, 0)))
    y = _ffn_call(h2, u_init, w['up'], w['ffn_cw'], w['ffn_cb'], w['down'], x1, mod3, cfg['tm_ffn'], 512, t_len)

    tail_rows = h2.reshape(bn, t_len, D_MODEL)[:, t_len - (FFN_CONV_W - 1):].reshape(bn * (FFN_CONV_W - 1), D_MODEL)
    n_tail = tail_rows.shape[0]
    tail_pad = -n_tail % SUBLANES_BF16
    u_tail = _mm_call(jnp.pad(tail_rows, ((0, tail_pad), (0, 0))), w['up'], n_tail + tail_pad, 1408, "ffn_state")
    u_tail = u_tail[:n_tail]
    new_ffn = jnp.concatenate([u_tail[:, :D_FF], u_tail[:, D_FF_PAD:D_FF_PAD + D_FF]], axis=1)
    new_ffn = new_ffn.reshape(bn, FFN_CONV_W - 1, 2 * D_FF)

    return (y.reshape(bn, t_len, D_MODEL), k_new.reshape(bn, t_len, ATT_HEADS, ATT_HEAD_DIM),
            v_new.reshape(bn, t_len, ATT_HEADS, ATT_HEAD_DIM), logf, new_h, new_mconv, new_ffn)


PROMPT_CFG = dict(tm_norm=512, tm_proj=512, tm_v=512, tm_zx=1024, tq=1024, tk=512, n_diag=1, tb=1024, q=128, tm_ffn=512)
SAMPLE_CFG = dict(tm_norm=64, tm_proj=512, tm_v=64, tm_zx=512, tq=128, tk=512, n_diag=1, tb=64, q=64, tm_ffn=64)


def kernel(x_prompt, x_sample, c_prompt, c_sample, cache_fox_k, cache_fox_v, cache_fox_logf, state_ssm, state_mamba_conv, state_ffn_conv, norm1_w, norm2_w, w_ada, b_ada, w_in, m_conv_w, m_conv_b, m_dt_bias, m_a_log, m_d, m_norm_w, f_bias, q_norm_w, k_norm_w, w_proj_m, w_proj_f, w_out, w_up, ffn_conv_w, ffn_conv_b, w_down):
    bp, bs = x_prompt.shape[0], x_sample.shape[0]
    depth = w_in.shape[0]
    xp, xs = x_prompt, x_sample
    c_all = jnp.concatenate([c_prompt, c_sample], axis=0)
    c_all = jnp.pad(c_all, ((0, -(bp + bs) % SUBLANES_BF16), (0, 0)))
    outs_p = [[] for _ in range(6)]
    outs_s = [[] for _ in range(6)]
    for l in range(depth):
        p = {
            'norm1_w': norm1_w[l], 'norm2_w': norm2_w[l], 'w_in': w_in[l], 'm_conv_w': m_conv_w[l],
            'm_conv_b': m_conv_b[l], 'm_dt_bias': m_dt_bias[l], 'm_a_log': m_a_log[l], 'm_d': m_d[l],
            'm_norm_w': m_norm_w[l], 'f_bias': f_bias[l], 'q_norm_w': q_norm_w[l], 'k_norm_w': k_norm_w[l],
            'w_proj_m': w_proj_m[l], 'w_proj_f': w_proj_f[l], 'w_out': w_out[l], 'w_up': w_up[l],
            'ffn_conv_w': ffn_conv_w[l], 'ffn_conv_b': ffn_conv_b[l], 'w_down': w_down[l],
        }
        w = _prep_weights(p)
        mod = _mod_call(c_all, w_ada[l], b_ada[l])
        dt = xp.dtype
        res_p = _layer(xp, mod[:bp], None, None, None,
                       jnp.zeros((bp, SSD_HEADS, SSD_HEAD_DIM, SSD_STATE), dt),
                       jnp.zeros((bp, SSD_CONV_W - 1, SSD_CONV_DIM), dt),
                       jnp.zeros((bp, FFN_CONV_W - 1, 2 * D_FF), dt), p, w, PROMPT_CFG)
        res_s = _layer(xs, mod[bp:bp + bs], cache_fox_k[l], cache_fox_v[l], cache_fox_logf[l], state_ssm[l],
                       state_mamba_conv[l], state_ffn_conv[l], p, w, SAMPLE_CFG)
        xp, xs = res_p[0], res_s[0]
        for i in range(6):
            outs_p[i].append(res_p[i + 1])
            outs_s[i].append(res_s[i + 1])
    k_p, v_p, lf_p, ssm_p, mc_p, fc_p = [jnp.stack(o, axis=0) for o in outs_p]
    k_s, v_s, lf_s, ssm_s, mc_s, fc_s = [jnp.stack(o, axis=0) for o in outs_s]
    return (xp, xs, k_p, v_p, lf_p, ssm_p, mc_p, fc_p, k_s, v_s, lf_s, ssm_s, mc_s, fc_s)
```

```python
import functools

import jax
import jax.numpy as jnp
from jax import lax
from jax.experimental import pallas as pl
from jax.experimental.pallas import tpu as pltpu

F32, BF16 = jnp.float32, jnp.bfloat16

D_MODEL = 2048
EPS = 1e-6
N_MOD = 6
SSD_INNER = 4096
SSD_HEADS = 64
SSD_HEAD_DIM = 64
SSD_GROUPS = 8
SSD_STATE = 128
SSD_CONV_W = 4
SSD_CONV_DIM = SSD_INNER + 2 * SSD_GROUPS * SSD_STATE
SSD_GROUP_W = SSD_INNER // SSD_GROUPS
SSD_HEADS_PER_GROUP = SSD_HEADS // SSD_GROUPS
ATT_HEADS = 16
ATT_HEAD_DIM = 128
ATT_WIDTH = ATT_HEADS * ATT_HEAD_DIM
ATT_SCALE = ATT_HEAD_DIM ** -0.5
D_FF = 5504
D_FF_PAD = 5632
FFN_CONV_W = 3
LANES = 128
SUBLANES_F32 = 8
SUBLANES_BF16 = 16
VMEM_LIMIT_BYTES = 56 * 1024 * 1024

DT_LANES = SSD_HEADS
LOGF_LANE0 = SSD_HEADS
NEG_BIG = -1e30
LOG2E = 1.4426950408889634
VT_ROWS = ATT_HEAD_DIM + SUBLANES_BF16
EXP_UNDERFLOW = 104.0


def _cparams(semantics):
    return pltpu.CompilerParams(dimension_semantics=semantics, vmem_limit_bytes=VMEM_LIMIT_BYTES)


def _silu(v):
    return v * jax.nn.sigmoid(v)


def _split3(v):
    hi = v.astype(BF16)
    r1 = v - hi.astype(F32)
    mid = r1.astype(BF16)
    lo = (r1 - mid.astype(F32)).astype(BF16)
    return hi, mid, lo


def _mod_kernel(c_ref, w_ref, b_ref, o_ref):
    c = c_ref[...]
    o_ref[...] = jnp.dot(_silu(c).astype(BF16), w_ref[...].astype(BF16),
                         preferred_element_type=F32) + b_ref[...]


def _mod_call(c_all, w_ada, b_ada):
    rows, n = c_all.shape[0], w_ada.shape[1]
    tn = 1024
    return pl.pallas_call(
        _mod_kernel,
        grid=(n // tn,),
        in_specs=[pl.BlockSpec((rows, D_MODEL), lambda j: (0, 0)),
                  pl.BlockSpec((D_MODEL, tn), lambda j: (0, j)),
                  pl.BlockSpec((1, tn), lambda j: (0, j))],
        out_specs=pl.BlockSpec((rows, tn), lambda j: (0, j)),
        out_shape=jax.ShapeDtypeStruct((rows, n), F32),
        compiler_params=_cparams(("arbitrary",)),
        name="ada_mod",
    )(c_all, w_ada, b_ada.reshape(1, n))


def _modulated_rms(x, nw, sc, sh):
    r = lax.rsqrt(jnp.mean(x * x, axis=-1, keepdims=True) + EPS)
    return (x * r) * nw * (1.0 + sc) + sh


def _prenorm_kernel(x_ref, nw_ref, sh_ref, sc_ref, o_ref):
    o_ref[...] = _modulated_rms(x_ref[...], nw_ref[...], sc_ref[...], sh_ref[...]).astype(BF16)


def _prenorm_call(x2, mod3, norm_w, seg_shift, seg_scale, tm, rows_per_batch):
    m = x2.shape[0]
    tpb = rows_per_batch // tm
    return pl.pallas_call(
        _prenorm_kernel,
        grid=(m // tm,),
        in_specs=[pl.BlockSpec((tm, D_MODEL), lambda i: (i, 0)),
                  pl.BlockSpec((1, D_MODEL), lambda i: (0, 0)),
                  pl.BlockSpec((None, 1, D_MODEL), lambda i: (i // tpb, 0, seg_shift)),
                  pl.BlockSpec((None, 1, D_MODEL), lambda i: (i // tpb, 0, seg_scale))],
        out_specs=pl.BlockSpec((tm, D_MODEL), lambda i: (i, 0)),
        out_shape=jax.ShapeDtypeStruct((m, D_MODEL), BF16),
        compiler_params=_cparams(("arbitrary",)),
        name="prenorm",
    )(x2, norm_w.reshape(1, D_MODEL), mod3, mod3)


def _mm_kernel(a_ref, w_ref, o_ref):
    o_ref[...] = jnp.dot(a_ref[...], w_ref[...], preferred_element_type=F32)


def _mm_call(a, w, tm, tn, name):
    m, k = a.shape
    n = w.shape[1]
    return pl.pallas_call(
        _mm_kernel,
        grid=(m // tm, n // tn),
        in_specs=[pl.BlockSpec((tm, k), lambda i, j: (i, 0)),
                  pl.BlockSpec((k, tn), lambda i, j: (0, j))],
        out_specs=pl.BlockSpec((tm, tn), lambda i, j: (i, j)),
        out_shape=jax.ShapeDtypeStruct((m, n), F32),
        compiler_params=_cparams(("arbitrary", "arbitrary")),
        name=name,
    )(a, w)


def _dt_logf(h, w_small, b_small):
    v = jnp.dot(h, w_small, preferred_element_type=F32) + b_small
    lane = lax.broadcasted_iota(jnp.int32, v.shape, 1)
    tail = jnp.log1p(jnp.exp(-jnp.abs(v)))
    softplus = jnp.maximum(v, 0.0) + tail
    log_sigmoid = jnp.minimum(v, 0.0) - tail
    return jnp.where(lane < DT_LANES, softplus, jnp.where(lane < LOGF_LANE0 + ATT_HEADS, log_sigmoid, 0.0))


CUM_ROWS = 64


def _tri(n, lower):
    r = lax.broadcasted_iota(jnp.int32, (n, n), 0)
    c = lax.broadcasted_iota(jnp.int32, (n, n), 1)
    return (c <= r) if lower else (r <= c)


def _cumsum_rows(x, tril3):
    hi, mid, lo = _split3(x)
    return jnp.dot(tril3, jnp.concatenate([hi, mid, lo], axis=0), preferred_element_type=F32)


def _cumsum_kernel(x_ref, o_ref, *, n_chunks):
    tril = _tri(CUM_ROWS, True).astype(BF16)
    tril3 = jnp.concatenate([tril, tril, tril], axis=1)

    def body(c, carry):
        s = pl.multiple_of(c * CUM_ROWS, CUM_ROWS)
        y = _cumsum_rows(x_ref[pl.ds(s, CUM_ROWS), :], tril3) + carry
        o_ref[pl.ds(s, CUM_ROWS), :] = y
        return y[CUM_ROWS - 1:CUM_ROWS, :]

    lax.fori_loop(0, n_chunks, body, jnp.zeros((1, LANES), F32))


def _cumsum_call(x3):
    bn, length, _ = x3.shape
    return pl.pallas_call(
        functools.partial(_cumsum_kernel, n_chunks=length // CUM_ROWS),
        grid=(bn,),
        in_specs=[pl.BlockSpec((None, length, LANES), lambda b: (b, 0, 0))],
        out_specs=pl.BlockSpec((None, length, LANES), lambda b: (b, 0, 0)),
        out_shape=jax.ShapeDtypeStruct(x3.shape, F32),
        compiler_params=_cparams(("arbitrary",)),
        name="logf_cumsum",
    )(x3)


def _head_rms(blk, w_row):
    ms = jnp.mean(blk * blk, axis=-1, keepdims=True)
    return blk * lax.rsqrt(ms + EPS) * w_row


def _key_bias_columns(cum, head, rows):
    lane = lax.broadcasted_iota(jnp.int32, (rows, LANES), 1)
    c = jnp.broadcast_to(cum[:, LOGF_LANE0 + head:LOGF_LANE0 + head + 1] * (-LOG2E), (rows, LANES))
    hi, mid, lo = _split3(c)
    return jnp.where(lane == 0, hi.astype(F32),
                     jnp.where(lane == 1, mid.astype(F32),
                               jnp.where(lane == 2, lo.astype(F32), 0.0)))


def _q_kernel(h_ref, w_ref, nw_ref, ws_ref, bs_ref, q_ref, dtf_ref):
    dtf_ref[...] = _dt_logf(h_ref[...], ws_ref[...], bs_ref[...])
    acc = jnp.dot(h_ref[...], w_ref[...], preferred_element_type=F32)
    for hh in range(ATT_HEADS):
        sl = slice(hh * ATT_HEAD_DIM, (hh + 1) * ATT_HEAD_DIM)
        q_ref[:, sl] = (_head_rms(acc[:, sl], nw_ref[...]) * (ATT_SCALE * LOG2E)).astype(BF16)


def _k_kernel(h_ref, w_ref, nw_ref, cum_ref, k_ref, kaug_ref):
    acc = jnp.dot(h_ref[...], w_ref[...], preferred_element_type=F32)
    rows = acc.shape[0]
    cum = cum_ref[...]
    for hh in range(ATT_HEADS):
        sl = slice(hh * ATT_HEAD_DIM, (hh + 1) * ATT_HEAD_DIM)
        kn = _head_rms(acc[:, sl], nw_ref[...])
        k_ref[:, sl] = kn
        kaug_ref[:, 2 * hh * LANES:(2 * hh + 1) * LANES] = kn.astype(BF16)
        kaug_ref[:, (2 * hh + 1) * LANES:(2 * hh + 2) * LANES] = _key_bias_columns(cum, hh, rows).astype(BF16)


def _kaug_kernel(k_ref, cum_ref, kaug_ref):
    rows = k_ref.shape[0]
    cum = cum_ref[...]
    for hh in range(ATT_HEADS):
        sl = slice(hh * ATT_HEAD_DIM, (hh + 1) * ATT_HEAD_DIM)
        kaug_ref[:, 2 * hh * LANES:(2 * hh + 1) * LANES] = k_ref[:, sl].astype(BF16)
        kaug_ref[:, (2 * hh + 1) * LANES:(2 * hh + 2) * LANES] = _key_bias_columns(cum, hh, rows).astype(BF16)


def _v_kernel(h_ref, w_ref, v_ref, vt_ref):
    acc = jnp.dot(h_ref[...], w_ref[...], preferred_element_type=F32)
    tm = acc.shape[0]
    v_ref[...] = acc
    row = lax.broadcasted_iota(jnp.int32, (VT_ROWS - ATT_HEAD_DIM, tm), 0)
    ones_rows = jnp.where(row == 0, 1.0, 0.0).astype(BF16)
    for hh in range(ATT_HEADS):
        vt_ref[hh, 0:ATT_HEAD_DIM, :] = acc[:, hh * ATT_HEAD_DIM:(hh + 1) * ATT_HEAD_DIM].T.astype(BF16)
        vt_ref[hh, ATT_HEAD_DIM:VT_ROWS, :] = ones_rows


def _row_spec(tm, n):
    return pl.BlockSpec((tm, n), lambda i: (i, 0))


def _const_spec(r, n):
    return pl.BlockSpec((r, n), lambda i: (0, 0))


def _q_call(h, w, nw, w_small, b_small, tm):
    m = h.shape[0]
    return pl.pallas_call(
        _q_kernel, grid=(m // tm,),
        in_specs=[_row_spec(tm, D_MODEL), _const_spec(D_MODEL, ATT_WIDTH), _const_spec(1, ATT_HEAD_DIM),
                  _const_spec(D_MODEL, LANES), _const_spec(1, LANES)],
        out_specs=[_row_spec(tm, ATT_WIDTH), _row_spec(tm, LANES)],
        out_shape=[jax.ShapeDtypeStruct((m, ATT_WIDTH), BF16), jax.ShapeDtypeStruct((m, LANES), F32)],
        compiler_params=_cparams(("arbitrary",)), name="q_proj",
    )(h, w, nw, w_small, b_small)


def _k_call(h, w, nw, cum2, tm):
    m = h.shape[0]
    return pl.pallas_call(
        _k_kernel, grid=(m // tm,),
        in_specs=[_row_spec(tm, D_MODEL), _const_spec(D_MODEL, ATT_WIDTH), _const_spec(1, ATT_HEAD_DIM),
                  _row_spec(tm, LANES)],
        out_specs=[_row_spec(tm, ATT_WIDTH), _row_spec(tm, 2 * ATT_WIDTH)],
        out_shape=[jax.ShapeDtypeStruct((m, ATT_WIDTH), F32), jax.ShapeDtypeStruct((m, 2 * ATT_WIDTH), BF16)],
        compiler_params=_cparams(("arbitrary",)), name="k_proj",
    )(h, w, nw, cum2)


def _kaug_call(k2, cum2, tm):
    m = k2.shape[0]
    return pl.pallas_call(
        _kaug_kernel, grid=(m // tm,),
        in_specs=[_row_spec(tm, ATT_WIDTH), _row_spec(tm, LANES)],
        out_specs=_row_spec(tm, 2 * ATT_WIDTH),
        out_shape=jax.ShapeDtypeStruct((m, 2 * ATT_WIDTH), BF16),
        compiler_params=_cparams(("arbitrary",)), name="k_cache_aug",
    )(k2, cum2)


def _v_call(h, w, tm, bn, t_len):
    m = h.shape[0]
    tpb = t_len // tm
    return pl.pallas_call(
        _v_kernel, grid=(m // tm,),
        in_specs=[_row_spec(tm, D_MODEL), _const_spec(D_MODEL, ATT_WIDTH)],
        out_specs=[_row_spec(tm, ATT_WIDTH),
                   pl.BlockSpec((None, ATT_HEADS, VT_ROWS, tm), lambda i: (i // tpb, 0, 0, i % tpb))],
        out_shape=[jax.ShapeDtypeStruct((m, ATT_WIDTH), F32),
                   jax.ShapeDtypeStruct((bn, ATT_HEADS, VT_ROWS, t_len), BF16)],
        compiler_params=_cparams(("arbitrary",)), name="v_proj",
    )(h, w)


def _attn_kernel(js_ref, q_ref, k_ref, vt_ref, o_ref, acc_ref, s_ref, s2_ref, *, tq, tk, n_diag, past):
    b, h, i = pl.program_id(0), pl.program_id(1), pl.program_id(2)
    nq = pl.num_programs(2)
    lane = lax.broadcasted_iota(jnp.int32, (tq, LANES), 1)
    ones = jnp.where(lane < 3, 1.0, 0.0).astype(BF16)
    qa = jnp.concatenate([q_ref[...], ones], axis=1)
    acc_ref[...] = jnp.zeros(acc_ref.shape, F32)

    def block(kb, vt_blk, m, c0, masked):
        n = kb.shape[0]
        st = lax.dot_general(kb, qa[c0:, :], (((1,), (1,)), ((), ())), preferred_element_type=F32)
        if masked:
            r = lax.broadcasted_iota(jnp.int32, st.shape, 0)
            c = lax.broadcasted_iota(jnp.int32, st.shape, 1)
            st = jnp.where(r <= c, st, NEG_BIG)
        s_ref[0:n, c0:] = st
        m_new = jnp.maximum(m, jnp.max(st, axis=0, keepdims=True))
        alpha = jnp.exp2(m - m_new)
        p = jnp.exp2(s_ref[0:n, c0:] - m_new).astype(BF16)
        pv = jnp.dot(vt_blk, p, preferred_element_type=F32)
        acc_ref[:, c0:] = alpha * acc_ref[:, c0:] + pv
        return m_new

    def pair(jj, m):
        s0 = pl.multiple_of(jj * 2 * tk, 2 * tk)
        sts = []
        for u, buf in enumerate((s_ref, s2_ref)):
            st = lax.dot_general(k_ref[pl.ds(s0 + u * tk, tk), :], qa, (((1,), (1,)), ((), ())),
                                 preferred_element_type=F32)
            buf[0:tk, :] = st
            sts.append(jnp.max(st, axis=0, keepdims=True))
        for u, buf in enumerate((s_ref, s2_ref)):
            m_new = jnp.maximum(m, sts[u])
            alpha = jnp.exp2(m - m_new)
            p = jnp.exp2(buf[0:tk, :] - m_new).astype(BF16)
            pv = jnp.dot(vt_ref[:, pl.ds(s0 + u * tk, tk)], p, preferred_element_type=F32)
            acc_ref[...] = alpha * acc_ref[...] + pv
            m = m_new
        return m

    j_start = js_ref[(b * ATT_HEADS + h) * nq + i]
    n_off = past // tk + i * (tq // tk)
    m = lax.fori_loop(j_start // 2, n_off // 2, pair, jnp.full((1, tq), NEG_BIG, F32))

    hw = tq // n_diag
    for d in range(n_diag):
        d0 = pl.multiple_of(past + i * tq + d * hw, hw)
        m_hi = block(k_ref[pl.ds(d0, hw), :], vt_ref[:, pl.ds(d0, hw)], m[:, d * hw:], d * hw, True)
        m = m_hi if d == 0 else jnp.concatenate([m[:, :d * hw], m_hi], axis=1)

    acc = acc_ref[...]
    o = acc[0:ATT_HEAD_DIM, :] * (1.0 / acc[ATT_HEAD_DIM:ATT_HEAD_DIM + 1, :])
    o_ref[...] = o.T.astype(BF16)


def _attn_block_starts(cum_all, q_norm_w, k_norm_w, tq, tk, past, nq):
    bn = cum_all.shape[0]
    bound = ATT_HEAD_DIM * jnp.max(jnp.abs(q_norm_w)) * jnp.max(jnp.abs(k_norm_w)) * ATT_SCALE
    thr = EXP_UNDERFLOW + 2.0 * bound
    c = cum_all[:, :, LOGF_LANE0:LOGF_LANE0 + ATT_HEADS]
    n_off = past // tk + jnp.arange(nq) * (tq // tk)
    nb = past // tk + (nq - 1) * (tq // tk)
    if nb == 0:
        return jnp.zeros((bn * ATT_HEADS * nq,), jnp.int32)
    t_real = c.shape[1] - past
    cq = jnp.pad(c[:, past:], ((0, 0), (0, nq * tq - t_real), (0, 0)), constant_values=-jnp.inf)
    cq_max = jnp.max(cq.reshape(bn, nq, tq, ATT_HEADS), axis=2)
    ck_min = jnp.min(c[:, :nb * tk].reshape(bn, nb, tk, ATT_HEADS), axis=2)
    gap = cq_max[:, :, None, :] - ck_min[:, None, :, :]
    jj = jnp.arange(nb)[None, None, :, None]
    cand = jnp.where((gap >= -thr) & (jj < n_off[None, :, None, None]), jj, n_off[None, :, None, None])
    js = jnp.min(cand, axis=2)
    return js.transpose(0, 2, 1).reshape(-1).astype(jnp.int32)


def _attn_call(js, q3, kaug3, vt4, tq, tk, n_diag, past):
    bn, t_q, _ = q3.shape
    t_k = kaug3.shape[1]
    grid_spec = pltpu.PrefetchScalarGridSpec(
        num_scalar_prefetch=1,
        grid=(bn, ATT_HEADS, t_q // tq),
        in_specs=[pl.BlockSpec((None, tq, ATT_HEAD_DIM), lambda b, h, i, js_ref: (b, i, h)),
                  pl.BlockSpec((None, t_k, 2 * ATT_HEAD_DIM), lambda b, h, i, js_ref: (b, 0, h)),
                  pl.BlockSpec((None, None, VT_ROWS, t_k), lambda b, h, i, js_ref: (b, h, 0, 0))],
        out_specs=pl.BlockSpec((None, tq, ATT_HEAD_DIM), lambda b, h, i, js_ref: (b, i, h)),
        scratch_shapes=[pltpu.VMEM((VT_ROWS, tq), F32), pltpu.VMEM((max(tk, tq // n_diag), tq), F32),
                        pltpu.VMEM((tk, tq), F32)])
    return pl.pallas_call(
        functools.partial(_attn_kernel, tq=tq, tk=tk, n_diag=n_diag, past=past),
        grid_spec=grid_spec,
        out_shape=jax.ShapeDtypeStruct((bn, t_q, ATT_WIDTH), BF16),
        compiler_params=_cparams(("arbitrary", "arbitrary", "arbitrary")),
        name="fox_attention",
    )(js, q3, kaug3, vt4)


def _ssd_kernel(x_ref, b_ref, c_ref, z_ref, dt_ref, csx_ref, csb_ref, csc_ref, h0_ref,
                cwx_ref, cwb_ref, cwc_ref, cbx_ref, cbb_ref, cbc_ref, alog_ref, d_ref, nw_ref,
                y_ref, hout_ref,
                ext_x, ext_b, ext_c, h_s, *, tb, q, n_chunks, n_tblocks):
    g = pl.program_id(1)
    t = pl.program_id(2)
    halo = SUBLANES_F32
    n_slabs = SSD_GROUP_W // LANES

    @pl.when(t == 0)
    def _():
        ext_x[0:halo, :] = csx_ref[...]
        ext_b[0:halo, :] = csb_ref[...]
        ext_c[0:halo, :] = csc_ref[...]
        h_s[...] = h0_ref[...]

    ext_x[halo:halo + tb, :] = x_ref[...]
    ext_b[halo:halo + tb, :] = b_ref[...]
    ext_c[halo:halo + tb, :] = c_ref[...]

    def conv(ext, cw_ref, cb_ref, s):
        e = ext[pl.ds(s, q + halo), :]
        y = cb_ref[...]
        for j in range(SSD_CONV_W):
            back = SSD_CONV_W - 1 - j
            win = e if back == 0 else pltpu.roll(e, back, 0)
            y = y + win[halo:halo + q, :] * cw_ref[j:j + 1, :]
        return _silu(y)

    lane_row = lax.broadcasted_iota(jnp.int32, (1, LANES), 1)
    a_row = jnp.where(lane_row < DT_LANES, -jnp.exp(alog_ref[...]), 0.0)
    shift = lax.rem(LANES - SSD_HEADS_PER_GROUP * g, LANES)
    tril_mask = _tri(q, True)
    tril = tril_mask.astype(BF16)
    tril3 = jnp.concatenate([tril, tril, tril], axis=1)
    lo_half = lax.broadcasted_iota(jnp.int32, (q, LANES), 1) < SSD_HEAD_DIM

    def chunk(ci, carry):
        s = pl.multiple_of(ci * q, q)
        dt_all = dt_ref[pl.ds(s, q), :]
        cum_all = _cumsum_rows(dt_all * a_row, tril3)
        cum_g = pltpu.roll(cum_all, shift, 1)
        dt_g = pltpu.roll(dt_all, shift, 1)
        cum_t = cum_g.T

        xc = conv(ext_x, cwx_ref, cbx_ref, s)
        bc = conv(ext_b, cwb_ref, cbb_ref, s).astype(BF16)
        cc = conv(ext_c, cwc_ref, cbc_ref, s).astype(BF16)
        cb = lax.dot_general(cc, bc, (((1,), (1,)), ((), ())), preferred_element_type=F32)
        h_t = h_s[...]
        y_inter = jnp.dot(cc, h_t.astype(BF16), preferred_element_type=F32)

        y_slabs, xd_slabs, etot_slabs = [], [], []
        for k in range(n_slabs):
            h0i, h1i = 2 * k, 2 * k + 1
            xck = xc[:, k * LANES:(k + 1) * LANES]
            cum_b = {hh: jnp.broadcast_to(cum_g[:, hh:hh + 1], (q, LANES)) for hh in (h0i, h1i)}
            dt_b = {hh: jnp.broadcast_to(dt_g[:, hh:hh + 1], (q, LANES)) for hh in (h0i, h1i)}
            cum_pair = jnp.where(lo_half, cum_b[h0i], cum_b[h1i])
            last_pair = cum_pair[q - 1:q, :]
            xdt = xck * jnp.where(lo_half, dt_b[h0i], dt_b[h1i])
            y_slab = y_inter[:, k * LANES:(k + 1) * LANES] * jnp.exp(cum_pair) + xck * d_ref[:, k * LANES:(k + 1) * LANES]
            for hi_, sel in ((h0i, lo_half), (h1i, jnp.logical_not(lo_half))):
                seg = cum_b[hi_][:, 0:q] - cum_t[hi_:hi_ + 1, :]
                lmat = jnp.exp(jnp.where(tril_mask, seg, -jnp.inf))
                mh = (cb * lmat).astype(BF16)
                rhs = jnp.where(sel, xdt, 0.0).astype(BF16)
                y_slab = y_slab + jnp.dot(mh, rhs, preferred_element_type=F32)
            y_slabs.append(y_slab)
            xd_slabs.append((xdt * jnp.exp(last_pair - cum_pair)).astype(BF16))
            etot_slabs.append(jnp.exp(last_pair))

        xd = jnp.concatenate(xd_slabs, axis=1)
        h_s[...] = h_t * jnp.concatenate(etot_slabs, axis=1) + lax.dot_general(
            bc, xd, (((0,), (0,)), ((), ())), preferred_element_type=F32)

        y = jnp.concatenate(y_slabs, axis=1) * _silu(z_ref[pl.ds(s, q), :])
        y = y * lax.rsqrt(jnp.mean(y * y, axis=-1, keepdims=True) + EPS) * nw_ref[...]
        y_ref[pl.ds(s, q), :] = y.astype(BF16)
        return carry

    lax.fori_loop(0, n_chunks, chunk, 0, unroll=min(2, n_chunks))

    for ext in (ext_x, ext_b, ext_c):
        ext[0:halo, :] = ext[tb:tb + halo, :]

    @pl.when(t == n_tblocks - 1)
    def _():
        hout_ref[...] = h_s[...]


def _ssd_call(zx3, dtf3, cs8, h0t, conv_w, conv_b, a_log_row, d_row, norm_w_row, tb, q):
    bn, t_len, _ = zx3.shape
    n_tblocks = t_len // tb
    gw, st = SSD_GROUP_W, SSD_STATE
    zx_x0, zx_b0, zx_c0 = SSD_INNER // gw, 2 * SSD_INNER // st, 2 * SSD_INNER // st + SSD_GROUPS
    cv_b0, cv_c0 = SSD_INNER // st, SSD_INNER // st + SSD_GROUPS

    def rows(width, col0):
        return pl.BlockSpec((None, tb, width), lambda b, g, t: (b, t, col0 + g))

    def per_group(r, width, col0):
        return pl.BlockSpec((r, width), lambda b, g, t: (0, col0 + g))

    def tail(width, col0):
        return pl.BlockSpec((None, SUBLANES_F32, width), lambda b, g, t: (b, 0, col0 + g))

    in_specs = [
        rows(gw, zx_x0), rows(st, zx_b0), rows(st, zx_c0), rows(gw, 0),
        pl.BlockSpec((None, tb, LANES), lambda b, g, t: (b, t, 0)),
        tail(gw, 0), tail(st, cv_b0), tail(st, cv_c0),
        pl.BlockSpec((None, None, st, gw), lambda b, g, t: (b, g, 0, 0)),
        per_group(SSD_CONV_W, gw, 0), per_group(SSD_CONV_W, st, cv_b0), per_group(SSD_CONV_W, st, cv_c0),
        per_group(1, gw, 0), per_group(1, st, cv_b0), per_group(1, st, cv_c0),
        pl.BlockSpec((1, LANES), lambda b, g, t: (0, 0)),
        per_group(1, gw, 0), per_group(1, gw, 0),
    ]
    out_specs = [pl.BlockSpec((None, tb, gw), lambda b, g, t: (b, t, g)),
                 pl.BlockSpec((None, None, st, gw), lambda b, g, t: (b, g, 0, 0))]
    out_shape = [jax.ShapeDtypeStruct((bn, t_len, SSD_INNER), BF16),
                 jax.ShapeDtypeStruct((bn, SSD_GROUPS, st, gw), F32)]
    scratch = [pltpu.VMEM((tb + SUBLANES_F32, gw), F32), pltpu.VMEM((tb + SUBLANES_F32, st), F32),
               pltpu.VMEM((tb + SUBLANES_F32, st), F32), pltpu.VMEM((st, gw), F32)]
    return pl.pallas_call(
        functools.partial(_ssd_kernel, tb=tb, q=q, n_chunks=tb // q, n_tblocks=n_tblocks),
        grid=(bn, SSD_GROUPS, n_tblocks),
        in_specs=in_specs, out_specs=out_specs, out_shape=out_shape, scratch_shapes=scratch,
        compiler_params=_cparams(("arbitrary", "arbitrary", "arbitrary")),
        name="ssd_scan",
    )(zx3, zx3, zx3, zx3, dtf3, cs8, cs8, cs8, h0t, conv_w, conv_w, conv_w, conv_b, conv_b, conv_b,
      a_log_row, d_row, norm_w_row)


def _merge_kernel(h_ref, ym_ref, yf_ref, wgm_ref, wgf_ref, wm_ref, wf_ref, o_ref):
    h = h_ref[...]
    gm = jax.nn.sigmoid(jnp.dot(h, wgm_ref[...], preferred_element_type=F32))
    gf = jax.nn.sigmoid(jnp.dot(h, wgf_ref[...], preferred_element_type=F32))
    a = jnp.dot(ym_ref[...], wm_ref[...], preferred_element_type=F32)
    b = jnp.dot(yf_ref[...], wf_ref[...], preferred_element_type=F32)
    o_ref[...] = (gm * a + gf * b).astype(BF16)


def _merge_call(h, ymn, yf, w_gates, w_pm, w_pf, tm, tn):
    m = h.shape[0]
    n_tiles = D_MODEL // tn
    return pl.pallas_call(
        _merge_kernel, grid=(m // tm, n_tiles),
        in_specs=[pl.BlockSpec((tm, D_MODEL), lambda i, j: (i, 0)),
                  pl.BlockSpec((tm, SSD_INNER), lambda i, j: (i, 0)),
                  pl.BlockSpec((tm, ATT_WIDTH), lambda i, j: (i, 0)),
                  pl.BlockSpec((D_MODEL, tn), lambda i, j: (0, j)),
                  pl.BlockSpec((D_MODEL, tn), lambda i, j: (0, n_tiles + j)),
                  pl.BlockSpec((SSD_INNER, tn), lambda i, j: (0, j)),
                  pl.BlockSpec((ATT_WIDTH, tn), lambda i, j: (0, j))],
        out_specs=pl.BlockSpec((tm, tn), lambda i, j: (i, j)),
        out_shape=jax.ShapeDtypeStruct((m, D_MODEL), BF16),
        compiler_params=_cparams(("arbitrary", "arbitrary")), name="gated_merge",
    )(h, ymn, yf, w_gates, w_gates, w_pm, w_pf)


def _outproj_kernel(mix_ref, w_ref, x_ref, g_ref, nw_ref, sh_ref, sc_ref, x1_ref, h2_ref):
    y = jnp.dot(mix_ref[...], w_ref[...], preferred_element_type=F32)
    x1 = x_ref[...] + g_ref[...] * y
    x1_ref[...] = x1
    h2_ref[...] = _modulated_rms(x1, nw_ref[...], sc_ref[...], sh_ref[...]).astype(BF16)


def _outproj_call(mix, w_out, x2, mod3, norm2_w, tm, rows_per_batch):
    m = x2.shape[0]
    tpb = rows_per_batch // tm

    def mod_seg(seg):
        return pl.BlockSpec((None, 1, D_MODEL), lambda i: (i // tpb, 0, seg))

    return pl.pallas_call(
        _outproj_kernel, grid=(m // tm,),
        in_specs=[_row_spec(tm, D_MODEL), _const_spec(D_MODEL, D_MODEL), _row_spec(tm, D_MODEL),
                  mod_seg(2), _const_spec(1, D_MODEL), mod_seg(3), mod_seg(4)],
        out_specs=[_row_spec(tm, D_MODEL), _row_spec(tm, D_MODEL)],
        out_shape=[jax.ShapeDtypeStruct((m, D_MODEL), F32), jax.ShapeDtypeStruct((m, D_MODEL), BF16)],
        compiler_params=_cparams(("arbitrary",)), name="out_proj",
    )(mix, w_out, x2, mod3, norm2_w.reshape(1, D_MODEL), mod3, mod3)


FFN_HALO = SUBLANES_BF16
FFN_SUB_ROWS = 256


def _ffn_kernel(h_ref, halo_ref, uia_ref, uib_ref, wa_ref, wb_ref, cwa_ref, cwb_ref, cba_ref, cbb_ref,
                wd_ref, x_ref, g_ref, o_ref, acc_ref, exta_ref, extb_ref, *, tm, tiles_per_seq, nf):
    i = pl.program_id(0)
    f = pl.program_id(1)
    first = (i % tiles_per_seq) == 0

    @pl.when(f == 0)
    def _():
        acc_ref[...] = jnp.zeros(acc_ref.shape, F32)

    h = h_ref[...]
    hprev = halo_ref[...]

    for ext_ref, w_ref, ui_ref in ((exta_ref, wa_ref, uia_ref), (extb_ref, wb_ref, uib_ref)):
        uh = jnp.dot(hprev, w_ref[...], preferred_element_type=F32)
        ext_ref[0:FFN_HALO, :] = jnp.where(first, ui_ref[...], uh)
        ext_ref[FFN_HALO:FFN_HALO + tm, :] = jnp.dot(h, w_ref[...], preferred_element_type=F32)

    def conv(ext_ref, cw_ref, cb_ref, r0, r1):
        y = cb_ref[...]
        for j in range(FFN_CONV_W):
            off = FFN_HALO - (FFN_CONV_W - 1) + j
            y = y + ext_ref[off + r0:off + r1, :] * cw_ref[j:j + 1, :]
        return y

    ua = _silu(conv(exta_ref, cwa_ref, cba_ref, 0, tm))
    rt = min(tm, FFN_SUB_ROWS)
    for r0 in range(0, tm, rt):
        ub = conv(extb_ref, cwb_ref, cbb_ref, r0, r0 + rt)
        gact = (ua[r0:r0 + rt, :] * ub).astype(BF16)
        acc_ref[r0:r0 + rt, :] += jnp.dot(gact, wd_ref[...], preferred_element_type=F32)

    @pl.when(f == nf - 1)
    def _():
        o_ref[...] = x_ref[...] + g_ref[...] * acc_ref[...]


def _ffn_call(h2, u_init, w_up_p, cw_p, cb_p, w_down_p, x1, mod3, tm, tf, rows_per_seq):
    m = h2.shape[0]
    nf = D_FF_PAD // tf
    tps = rows_per_seq // tm
    halo_blocks = tm // FFN_HALO
    return pl.pallas_call(
        functools.partial(_ffn_kernel, tm=tm, tiles_per_seq=tps, nf=nf),
        grid=(m // tm, nf),
        in_specs=[pl.BlockSpec((tm, D_MODEL), lambda i, f: (i, 0)),
                  pl.BlockSpec((FFN_HALO, D_MODEL), lambda i, f: (jnp.maximum(i * halo_blocks - 1, 0), 0)),
                  pl.BlockSpec((None, FFN_HALO, tf), lambda i, f: (i // tps, 0, f)),
                  pl.BlockSpec((None, FFN_HALO, tf), lambda i, f: (i // tps, 0, nf + f)),
                  pl.BlockSpec((D_MODEL, tf), lambda i, f: (0, f)),
                  pl.BlockSpec((D_MODEL, tf), lambda i, f: (0, nf + f)),
                  pl.BlockSpec((FFN_CONV_W, tf), lambda i, f: (0, f)),
                  pl.BlockSpec((FFN_CONV_W, tf), lambda i, f: (0, nf + f)),
                  pl.BlockSpec((1, tf), lambda i, f: (0, f)),
                  pl.BlockSpec((1, tf), lambda i, f: (0, nf + f)),
                  pl.BlockSpec((tf, D_MODEL), lambda i, f: (f, 0)),
                  pl.BlockSpec((tm, D_MODEL), lambda i, f: (i, 0)),
                  pl.BlockSpec((None, 1, D_MODEL), lambda i, f: (i // tps, 0, 5))],
        out_specs=pl.BlockSpec((tm, D_MODEL), lambda i, f: (i, 0)),
        out_shape=jax.ShapeDtypeStruct((m, D_MODEL), F32),
        scratch_shapes=[pltpu.VMEM((tm, D_MODEL), F32), pltpu.VMEM((tm + FFN_HALO, tf), F32),
                        pltpu.VMEM((tm + FFN_HALO, tf), F32)],
        compiler_params=_cparams(("arbitrary", "arbitrary")), name="conv_mlp",
    )(h2, h2, u_init, u_init, w_up_p, w_up_p, cw_p, cw_p, cb_p, cb_p, w_down_p, x1, mod3)


def _pad_ff(a, axis):
    lo, hi = jnp.split(a, 2, axis=axis)
    pad = [(0, 0)] * a.ndim
    pad[axis] = (0, D_FF_PAD - D_FF)
    return jnp.concatenate([jnp.pad(lo, pad), jnp.pad(hi, pad)], axis=axis)


def _prep_weights(p):
    w_in = p['w_in']
    o_z, o_xbc, o_dt = 0, SSD_INNER, SSD_INNER + SSD_CONV_DIM
    o_q = o_dt + SSD_HEADS
    o_k, o_v = o_q + ATT_WIDTH, o_q + 2 * ATT_WIDTH
    o_f = o_q + 3 * ATT_WIDTH
    o_g = o_f + ATT_HEADS
    pad_small = LANES - SSD_HEADS - ATT_HEADS
    w = {
        'zx': w_in[:, o_z:o_dt].astype(BF16),
        'small': jnp.concatenate([w_in[:, o_dt:o_q], w_in[:, o_f:o_g],
                                  jnp.zeros((D_MODEL, pad_small), F32)], axis=1).astype(BF16),
        'b_small': jnp.concatenate([p['m_dt_bias'], p['f_bias'], jnp.zeros((pad_small,), F32)]).reshape(1, LANES),
        'q': w_in[:, o_q:o_k].astype(BF16),
        'k': w_in[:, o_k:o_v].astype(BF16),
        'v': w_in[:, o_v:o_f].astype(BF16),
        'gates': w_in[:, o_g:o_g + 2 * D_MODEL].astype(BF16),
        'pm': p['w_proj_m'].astype(BF16),
        'pf': p['w_proj_f'].astype(BF16),
        'out': p['w_out'].astype(BF16),
        'up': _pad_ff(p['w_up'], 1).astype(BF16),
        'ffn_cw': _pad_ff(p['ffn_conv_w'], 1),
        'ffn_cb': _pad_ff(p['ffn_conv_b'].reshape(1, 2 * D_FF), 1),
        'down': jnp.pad(p['w_down'], ((0, D_FF_PAD - D_FF), (0, 0))).astype(BF16),
        'a_log': jnp.pad(p['m_a_log'], (0, LANES - SSD_HEADS)).reshape(1, LANES),
        'd_row': jnp.repeat(p['m_d'], SSD_HEAD_DIM).reshape(1, SSD_INNER),
    }
    return w


def _layer(x, mod, k_past, v_past, logf_past, ssm_h0, mconv_buf, ffn_buf, p, w, cfg):
    bn, t_len, _ = x.shape
    m = bn * t_len
    past = 0 if k_past is None else k_past.shape[1]
    mod3 = mod.reshape(bn, 1, N_MOD * D_MODEL)
    x2 = x.reshape(m, D_MODEL)

    h = _prenorm_call(x2, mod3, p['norm1_w'], 0, 1, cfg['tm_norm'], t_len)
    q, dtf = _q_call(h, w['q'], p['q_norm_w'].reshape(1, ATT_HEAD_DIM), w['small'], w['b_small'], cfg['tm_proj'])
    dtf3 = dtf.reshape(bn, t_len, LANES)
    logf = dtf3[:, :, LOGF_LANE0:LOGF_LANE0 + ATT_HEADS]

    if past:
        lf_past = jnp.pad(logf_past.astype(F32), ((0, 0), (0, 0), (LOGF_LANE0, LANES - LOGF_LANE0 - ATT_HEADS)))
        cum_all = _cumsum_call(jnp.concatenate([lf_past, dtf3], axis=1))
        cum_past, cum_new = cum_all[:, :past], cum_all[:, past:]
    else:
        cum_all = cum_new = _cumsum_call(dtf3)

    zx = _mm_call(h, w['zx'], cfg['tm_zx'], 1024, "zx_proj")
    k_new, kaug = _k_call(h, w['k'], p['k_norm_w'].reshape(1, ATT_HEAD_DIM), cum_new.reshape(m, LANES), cfg['tm_proj'])
    v_new, vt4 = _v_call(h, w['v'], cfg['tm_v'], bn, t_len)

    q3 = q.reshape(bn, t_len, ATT_WIDTH)
    kaug3 = kaug.reshape(bn, t_len, 2 * ATT_WIDTH)
    tq, tk = cfg['tq'], cfg['tk']
    tpad = tq - t_len if past else 0
    if past:
        kaug_past = _kaug_call(k_past.reshape(bn * past, ATT_WIDTH).astype(F32), cum_past.reshape(bn * past, LANES),
                               cfg['tm_proj']).reshape(bn, past, 2 * ATT_WIDTH)
        kaug3 = jnp.concatenate([kaug_past, jnp.pad(kaug3, ((0, 0), (0, tpad), (0, 0)))], axis=1)
        vt_past = v_past.astype(BF16).transpose(0, 2, 3, 1)
        ones_rows = jnp.zeros((bn, ATT_HEADS, VT_ROWS - ATT_HEAD_DIM, past), BF16).at[:, :, 0, :].set(1.0)
        vt4 = jnp.concatenate([jnp.concatenate([vt_past, ones_rows], axis=2),
                               jnp.pad(vt4, ((0, 0), (0, 0), (0, 0), (0, tpad)))], axis=3)
        q3 = jnp.pad(q3, ((0, 0), (0, tpad), (0, 0)))
    js = _attn_block_starts(cum_all, p['q_norm_w'], p['k_norm_w'], tq, tk, past, (t_len + tpad) // tq)
    yf = _attn_call(js, q3, kaug3, vt4, tq, tk, cfg['n_diag'], past)[:, :t_len].reshape(m, ATT_WIDTH)

    zx3 = zx.reshape(bn, t_len, SSD_INNER + SSD_CONV_DIM)
    cs8 = jnp.pad(mconv_buf.astype(F32), ((0, 0), (SUBLANES_F32 - (SSD_CONV_W - 1), 0), (0, 0)))
    hpg = SSD_HEADS_PER_GROUP
    h0t = ssm_h0.astype(F32).reshape(bn, SSD_GROUPS, hpg, SSD_HEAD_DIM, SSD_STATE)
    h0t = h0t.transpose(0, 1, 4, 2, 3).reshape(bn, SSD_GROUPS, SSD_STATE, SSD_GROUP_W)
    ymn, h_t = _ssd_call(zx3, dtf3, cs8, h0t, p['m_conv_w'], p['m_conv_b'].reshape(1, SSD_CONV_DIM),
                         w['a_log'], w['d_row'], p['m_norm_w'].reshape(1, SSD_INNER), cfg['tb'], cfg['q'])
    new_h = h_t.reshape(bn, SSD_GROUPS, SSD_STATE, hpg, SSD_HEAD_DIM).transpose(0, 1, 3, 4, 2)
    new_h = new_h.reshape(bn, SSD_HEADS, SSD_HEAD_DIM, SSD_STATE)
    new_mconv = zx3[:, t_len - (SSD_CONV_W - 1):, SSD_INNER:]

    mix = _merge_call(h, ymn.reshape(m, SSD_INNER), yf, w['gates'], w['pm'], w['pf'], cfg['tm_proj'], 512)
    x1, h2 = _outproj_call(mix, w['out'], x2, mod3, p['norm2_w'], cfg['tm_norm'], t_len)

    u_init = jnp.pad(_pad_ff(ffn_buf.astype(F32), 2), ((0, 0), (FFN_HALO - (FFN_CONV_W - 1), 0), (0, 0)))
    y = _ffn_call(h2, u_init, w['up'], w['ffn_cw'], w['ffn_cb'], w['down'], x1, mod3, cfg['tm_ffn'], 512, t_len)

    tail_rows = h2.reshape(bn, t_len, D_MODEL)[:, t_len - (FFN_CONV_W - 1):].reshape(bn * (FFN_CONV_W - 1), D_MODEL)
    n_tail = tail_rows.shape[0]
    tail_pad = -n_tail % SUBLANES_BF16
    u_tail = _mm_call(jnp.pad(tail_rows, ((0, tail_pad), (0, 0))), w['up'], n_tail + tail_pad, 1408, "ffn_state")
    u_tail = u_tail[:n_tail]
    new_ffn = jnp.concatenate([u_tail[:, :D_FF], u_tail[:, D_FF_PAD:D_FF_PAD + D_FF]], axis=1)
    new_ffn = new_ffn.reshape(bn, FFN_CONV_W - 1, 2 * D_FF)

    return (y.reshape(bn, t_len, D_MODEL), k_new.reshape(bn, t_len, ATT_HEADS, ATT_HEAD_DIM),
            v_new.reshape(bn, t_len, ATT_HEADS, ATT_HEAD_DIM), logf, new_h, new_mconv, new_ffn)


PROMPT_CFG = dict(tm_norm=512, tm_proj=512, tm_v=512, tm_zx=1024, tq=1024, tk=512, n_diag=1, tb=1024, q=128, tm_ffn=512)
SAMPLE_CFG = dict(tm_norm=64, tm_proj=512, tm_v=64, tm_zx=512, tq=128, tk=512, n_diag=1, tb=64, q=64, tm_ffn=64)


def kernel(x_prompt, x_sample, c_prompt, c_sample, cache_fox_k, cache_fox_v, cache_fox_logf, state_ssm, state_mamba_conv, state_ffn_conv, norm1_w, norm2_w, w_ada, b_ada, w_in, m_conv_w, m_conv_b, m_dt_bias, m_a_log, m_d, m_norm_w, f_bias, q_norm_w, k_norm_w, w_proj_m, w_proj_f, w_out, w_up, ffn_conv_w, ffn_conv_b, w_down):
    bp, bs = x_prompt.shape[0], x_sample.shape[0]
    depth = w_in.shape[0]
    xp, xs = x_prompt, x_sample
    c_all = jnp.concatenate([c_prompt, c_sample], axis=0)
    c_all = jnp.pad(c_all, ((0, -(bp + bs) % SUBLANES_BF16), (0, 0)))
    outs_p = [[] for _ in range(6)]
    outs_s = [[] for _ in range(6)]
    for l in range(depth):
        p = {
            'norm1_w': norm1_w[l], 'norm2_w': norm2_w[l], 'w_in': w_in[l], 'm_conv_w': m_conv_w[l],
            'm_conv_b': m_conv_b[l], 'm_dt_bias': m_dt_bias[l], 'm_a_log': m_a_log[l], 'm_d': m_d[l],
            'm_norm_w': m_norm_w[l], 'f_bias': f_bias[l], 'q_norm_w': q_norm_w[l], 'k_norm_w': k_norm_w[l],
            'w_proj_m': w_proj_m[l], 'w_proj_f': w_proj_f[l], 'w_out': w_out[l], 'w_up': w_up[l],
            'ffn_conv_w': ffn_conv_w[l], 'ffn_conv_b': ffn_conv_b[l], 'w_down': w_down[l],
        }
        w = _prep_weights(p)
        mod = _mod_call(c_all, w_ada[l], b_ada[l])
        dt = xp.dtype
        res_p = _layer(xp, mod[:bp], None, None, None,
                       jnp.zeros((bp, SSD_HEADS, SSD_HEAD_DIM, SSD_STATE), dt),
                       jnp.zeros((bp, SSD_CONV_W - 1, SSD_CONV_DIM), dt),
                       jnp.zeros((bp, FFN_CONV_W - 1, 2 * D_FF), dt), p, w, PROMPT_CFG)
        res_s = _layer(xs, mod[bp:bp + bs], cache_fox_k[l], cache_fox_v[l], cache_fox_logf[l], state_ssm[l],
                       state_mamba_conv[l], state_ffn_conv[l], p, w, SAMPLE_CFG)
        xp, xs = res_p[0], res_s[0]
        for i in range(6):
            outs_p[i].append(res_p[i + 1])
            outs_s[i].append(res_s[i + 1])
    k_p, v_p, lf_p, ssm_p, mc_p, fc_p = [jnp.stack(o, axis=0) for o in outs_p]
    k_s, v_s, lf_s, ssm_s, mc_s, fc_s = [jnp.stack(o, axis=0) for o in outs_s]
    return (xp, xs, k_p, v_p, lf_p, ssm_p, mc_p, fc_p, k_s, v_s, lf_s, ssm_s, mc_s, fc_s)
```
